```python
import math
import jax, jax.numpy as jnp
from jax import lax
import numpy as np

D_MODEL = 1024
BATCH = 8
SEQ = 4096
DEPTH = 1
DEC_BATCH = 32
DEC_SEQ = 1
PAST_LEN = 16384
PAGE_SIZE = 128

D_HEAD = 64
H_DIFF = 4
H_DSA = 8
H_IDX = 8
D_IDX = 64
MIX_WIDTH = H_DIFF * 2 * D_HEAD + H_DSA * D_HEAD
TOPK_KEYS = 256
N_EXPERTS = 32
TOP_K = 4
D_FF = 1024
SWIGLU_LIMIT = 7.0
SWIGLU_ALPHA = 1.702
QBLK = 128
MOE_BLK = 128
EPS = 1e-6
IN_SIZES = (H_DIFF * 2 * D_HEAD, H_DIFF * 2 * D_HEAD, H_DIFF * 2 * D_HEAD,
            H_DSA * D_HEAD, H_DSA * D_HEAD, H_DSA * D_HEAD,
            H_IDX * D_IDX, D_IDX, H_IDX)
N_IN = sum(IN_SIZES)

kernel_name = 'hybrid_diffattn_dsa_moe_step'


def rms_norm(x, g):
    xf = x.astype(jnp.float32)
    y = xf * lax.rsqrt(jnp.mean(xf * xf, axis=-1, keepdims=True) + EPS)
    return (y * g.astype(jnp.float32)).astype(x.dtype)


def alibi_slopes(n):
    return jnp.asarray([2.0 ** (-8.0 * (i + 1) / n) for i in range(n)], dtype=jnp.float32)


def lambda_init_for(layer_idx):
    return 0.8 - 0.6 * math.exp(-0.3 * layer_idx)


def diff_lambda(lam_p, lam_init):
    lp = lam_p.astype(jnp.float32)
    return jnp.exp(jnp.sum(lp[0] * lp[1])) - jnp.exp(jnp.sum(lp[2] * lp[3])) + lam_init


def ada_modulation(c, w_ada, b_ada):
    m = jax.nn.silu(c) @ w_ada + b_ada
    return jnp.split(m[:, None, :], 6, axis=-1)


def modulate(x, g, shift, scale):
    return rms_norm(x, g) * (1 + scale) + shift


def in_project(h, w_in, qg_diff, kg_diff, qg_dsa, kg_dsa):
    B, S, _ = h.shape
    splits = np.cumsum(IN_SIZES)[:-1].tolist()
    dq, dk, dv, sq, sk, sv, iq, ik, iw = jnp.split(h @ w_in, splits, axis=-1)
    dq = rms_norm(dq.reshape(B, S, H_DIFF, 2, D_HEAD), qg_diff)
    dk = rms_norm(dk.reshape(B, S, H_DIFF, 2, D_HEAD), kg_diff)
    dv = dv.reshape(B, S, H_DIFF, 2 * D_HEAD)
    sq = rms_norm(sq.reshape(B, S, H_DSA, D_HEAD), qg_dsa)
    sk = rms_norm(sk.reshape(B, S, H_DSA, D_HEAD), kg_dsa)
    sv = sv.reshape(B, S, H_DSA, D_HEAD)
    iq = iq.reshape(B, S, H_IDX, D_IDX)
    iw = iw * (H_IDX ** -0.5)
    return dq, dk, dv, sq, sk, sv, iq, ik, iw


def diff_scores(q, k, slopes, dist):
    s = jnp.einsum('bqhjd,bshjd->bhjqs', q, k).astype(jnp.float32) * (D_HEAD ** -0.5)
    s = s - slopes[None, :, None, None, None] * dist
    return jnp.where(dist >= 0, s, -jnp.inf)


def diff_attn_prompt(q, k, v, lam, slopes):
    B, S = q.shape[:2]
    nb = S // QBLK
    qb = q.reshape(B, nb, QBLK, H_DIFF, 2, D_HEAD).swapaxes(0, 1)
    kpos = jnp.arange(S)

    def block(args):
        qi, i = args
        qpos = i * QBLK + jnp.arange(QBLK)
        dist = (qpos[:, None] - kpos[None, :]).astype(jnp.float32)
        p = jax.nn.softmax(diff_scores(qi, k, slopes, dist), axis=-1)
        pd = (p[:, :, 0] - lam * p[:, :, 1]).astype(v.dtype)
        return jnp.einsum('bhqs,bshe->bqhe', pd, v)

    o = lax.map(block, (qb, jnp.arange(nb)))
    return o.swapaxes(0, 1).reshape(B, S, H_DIFF, 2 * D_HEAD)


def diff_attn_sample(q, k_new, v_new, k_pool, v_pool, layer, page_table, lam, slopes):
    DB, DQ = q.shape[:2]
    k_past = k_pool[layer, page_table].reshape(DB, -1, H_DIFF, 2, D_HEAD)
    v_past = v_pool[layer, page_table].reshape(DB, -1, H_DIFF, 2 * D_HEAD)
    P = k_past.shape[1]
    qpos = P + jnp.arange(DQ)
    dist_past = (qpos[:, None] - jnp.arange(P)[None, :]).astype(jnp.float32)
    dist_new = (qpos[:, None] - qpos[None, :]).astype(jnp.float32)
    s = jnp.concatenate([diff_scores(q, k_past, slopes, dist_past),
                         diff_scores(q, k_new, slopes, dist_new)], axis=-1)
    p = jax.nn.softmax(s, axis=-1)
    pd = (p[:, :, 0] - lam * p[:, :, 1]).astype(v_new.dtype)
    return (jnp.einsum('bhqs,bshe->bqhe', pd[..., :P], v_past)
            + jnp.einsum('bhqs,bshe->bqhe', pd[..., P:], v_new))


def indexer_scores(iq, iw, ik, causal):
    r = jax.nn.relu(jnp.einsum('bqhd,bsd->bqhs', iq, ik))
    s = jnp.einsum('bqhs,bqh->bqs', r, iw).astype(jnp.float32)
    return jnp.where(causal[None], s, -jnp.inf)


def gather_rows(src, idx):
    return jax.vmap(lambda sb, ib: sb[ib])(src, idx)


def sparse_attend(q, k_sel, v_sel, dist, slopes):
    s = jnp.einsum('bqhd,bqkhd->bhqk', q, k_sel).astype(jnp.float32) * (D_HEAD ** -0.5)
    s = s - slopes[None, :, None, None] * dist[:, None]
    s = jnp.where(dist[:, None] >= 0, s, -jnp.inf)
    p = jax.nn.softmax(s, axis=-1).astype(v_sel.dtype)
    return jnp.einsum('bhqk,bqkhd->bqhd', p, v_sel)


def dsa_prompt(q, k, v, iq, ik, iw, slopes):
    B, S = q.shape[:2]
    n_sel = min(TOPK_KEYS, S // 4)
    nb = S // QBLK

    def to_blocks(a):
        return a.reshape((B, nb, QBLK) + a.shape[2:]).swapaxes(0, 1)

    kpos = jnp.arange(S)

    def block(args):
        qi, iqi, iwi, i = args
        qpos = i * QBLK + jnp.arange(QBLK)
        scores = indexer_scores(iqi, iwi, ik, kpos[None, :] <= qpos[:, None])
        _, sel = lax.top_k(scores, n_sel)
        dist = (qpos[None, :, None] - sel).astype(jnp.float32)
        return sparse_attend(qi, gather_rows(k, sel), gather_rows(v, sel), dist, slopes)

    o = lax.map(block, (to_blocks(q), to_blocks(iq), to_blocks(iw), jnp.arange(nb)))
    return o.swapaxes(0, 1).reshape(B, S, H_DSA, D_HEAD)


def dsa_sample(q, k_new, v_new, iq, ik_new, iw, k_pool, v_pool, ik_pool, layer, page_table, slopes):
    DB, DQ = q.shape[:2]
    page = ik_pool.shape[2]
    ik_past = ik_pool[layer, page_table].reshape(DB, -1, D_IDX)
    P = ik_past.shape[1]
    L = P + DQ
    n_sel = min(TOPK_KEYS, L // 4)
    qpos = P + jnp.arange(DQ)
    causal = jnp.arange(L)[None, :] <= qpos[:, None]
    scores = indexer_scores(iq, iw, jnp.concatenate([ik_past, ik_new], axis=1), causal)
    _, sel = lax.top_k(scores, n_sel)
    in_past = (sel < P)[..., None, None]
    sp = jnp.minimum(sel, P - 1)
    phys = jnp.take_along_axis(page_table, (sp // page).reshape(DB, -1), axis=1).reshape(sel.shape)
    off = sp % page
    sn = jnp.clip(sel - P, 0, DQ - 1)
    k_sel = jnp.where(in_past, k_pool[layer, phys, off], gather_rows(k_new, sn))
    v_sel = jnp.where(in_past, v_pool[layer, phys, off], gather_rows(v_new, sn))
    dist = (qpos[None, :, None] - sel).astype(jnp.float32)
    return sparse_attend(q, k_sel, v_sel, dist, slopes)


def moe_ffn(h, w_router, b_router, w_gu, b_gu, w_down, b_down):
    B, S, D = h.shape
    x = h.reshape(-1, D)
    T = x.shape[0]
    logits = (x @ w_router + b_router).astype(jnp.float32)
    top_logit, top_e = lax.top_k(logits, TOP_K)
    gate = jax.nn.softmax(top_logit, axis=-1)
    A = T * TOP_K
    flat_e = top_e.reshape(-1)
    order = jnp.argsort(flat_e)
    e_sorted = flat_e[order]
    tok_sorted = order // TOP_K
    gate_sorted = gate.reshape(-1)[order]
    counts = jnp.bincount(flat_e, length=N_EXPERTS)
    padded = (counts + MOE_BLK - 1) // MOE_BLK * MOE_BLK
    pad_end = jnp.cumsum(padded)
    pad_start = pad_end - padded
    grp_start = jnp.cumsum(counts) - counts
    dest = pad_start[e_sorted] + jnp.arange(A) - grp_start[e_sorted]
    n_blk = (A + N_EXPERTS * (MOE_BLK - 1) + MOE_BLK - 1) // MOE_BLK
    n_rows = n_blk * MOE_BLK
    row_tok = jnp.full((n_rows,), T, jnp.int32).at[dest].set(tok_sorted.astype(jnp.int32))
    row_gate = jnp.zeros((n_rows,), jnp.float32).at[dest].set(gate_sorted)
    blk_e = jnp.minimum(jnp.searchsorted(pad_end, jnp.arange(n_blk) * MOE_BLK, side='right'), N_EXPERTS - 1)
    xb = jnp.concatenate([x, jnp.zeros((1, D), x.dtype)], axis=0)[row_tok].reshape(n_blk, MOE_BLK, D)

    def expert_block(args):
        xi, e = args
        g, u = jnp.split(xi @ w_gu[e] + b_gu[e], 2, axis=-1)
        g = jnp.minimum(g, SWIGLU_LIMIT)
        u = jnp.clip(u, -SWIGLU_LIMIT, SWIGLU_LIMIT)
        return ((u + 1) * (g * jax.nn.sigmoid(SWIGLU_ALPHA * g))) @ w_down[e] + b_down[e]

    yb = lax.map(expert_block, (xb, blk_e)).reshape(n_rows, D)
    y = jax.ops.segment_sum(yb * row_gate[:, None].astype(yb.dtype), row_tok, num_segments=T + 1)[:T]
    return y.reshape(B, S, D)


def trunk_layer(x, c, lw, attend):
    (w_ada, b_ada, norm_attn_g, w_in, qg_diff, kg_diff, qg_dsa, kg_dsa, subln_g, lam_init,
     w_out, norm_ffn_g, w_router, b_router, w_gu, b_gu, w_down, b_down) = lw
    B, S, _ = x.shape
    sh_a, sc_a, g_a, sh_f, sc_f, g_f = ada_modulation(c, w_ada, b_ada)
    h = modulate(x, norm_attn_g, sh_a, sc_a)
    dq, dk, dv, sq, sk, sv, iq, ik, iw = in_project(h, w_in, qg_diff, kg_diff, qg_dsa, kg_dsa)
    o_diff, o_dsa = attend(dq, dk, dv, sq, sk, sv, iq, ik, iw)
    o_diff = rms_norm(o_diff, subln_g) * (1 - lam_init)
    o = jnp.concatenate([o_diff.reshape(B, S, -1), o_dsa.reshape(B, S, -1)], axis=-1) @ w_out
    x = x + g_a * o
    h = modulate(x, norm_ffn_g, sh_f, sc_f)
    x = x + g_f * moe_ffn(h, w_router, b_router, w_gu, b_gu, w_down, b_down)
    return x, (dk, dv, sk, sv, ik)


def setup_inputs(seed: int = 0) -> dict:
    key = jax.random.key(seed)
    ks = iter(jax.random.split(key, 32))
    n_pages = PAST_LEN // PAGE_SIZE
    n_used = DEC_BATCH * n_pages
    n_pool = n_used + max(1, n_used // 4)

    def nrm(shape, scale=1.0):
        return scale * jax.random.normal(next(ks), shape, jnp.float32)

    def gain(shape):
        return 1.0 + nrm(shape, 0.05)

    x_prompt = nrm((BATCH, SEQ, D_MODEL))
    x_sample = nrm((DEC_BATCH, DEC_SEQ, D_MODEL))
    cache_diff_k = nrm((DEPTH, n_pool, PAGE_SIZE, H_DIFF, 2, D_HEAD))
    cache_diff_v = nrm((DEPTH, n_pool, PAGE_SIZE, H_DIFF, 2 * D_HEAD))
    cache_dsa_k = nrm((DEPTH, n_pool, PAGE_SIZE, H_DSA, D_HEAD))
    cache_dsa_v = nrm((DEPTH, n_pool, PAGE_SIZE, H_DSA, D_HEAD))
    cache_idx_k = nrm((DEPTH, n_pool, PAGE_SIZE, D_IDX), 0.5)
    page_table = jax.random.permutation(next(ks), n_pool)[:n_used].reshape(DEC_BATCH, n_pages).astype(jnp.int32)
    return {
        'x_prompt': x_prompt,
        'x_sample': x_sample,
        'cache_diff_k': cache_diff_k,
        'cache_diff_v': cache_diff_v,
        'cache_dsa_k': cache_dsa_k,
        'cache_dsa_v': cache_dsa_v,
        'cache_idx_k': cache_idx_k,
        'page_table': page_table,
        'c_prompt': nrm((BATCH, D_MODEL)),
        'c_sample': nrm((DEC_BATCH, D_MODEL)),
        'w_ada': nrm((DEPTH, D_MODEL, 6 * D_MODEL), 0.5 * D_MODEL ** -0.5),
        'b_ada': nrm((DEPTH, 6 * D_MODEL), 0.02),
        'norm_attn_g': gain((DEPTH, D_MODEL)),
        'w_in': nrm((DEPTH, D_MODEL, N_IN), D_MODEL ** -0.5),
        'q_norm_diff_g': gain((DEPTH, D_HEAD)),
        'k_norm_diff_g': gain((DEPTH, D_HEAD)),
        'q_norm_dsa_g': gain((DEPTH, D_HEAD)),
        'k_norm_dsa_g': gain((DEPTH, D_HEAD)),
        'diff_lambda_qk': nrm((DEPTH, 4, D_HEAD), 0.1),
        'diff_subln_g': gain((DEPTH, 2 * D_HEAD)),
        'w_out': nrm((DEPTH, MIX_WIDTH, D_MODEL), MIX_WIDTH ** -0.5),
        'norm_ffn_g': gain((DEPTH, D_MODEL)),
        'w_router': nrm((DEPTH, D_MODEL, N_EXPERTS), D_MODEL ** -0.5),
        'b_router': nrm((DEPTH, N_EXPERTS), 0.01),
        'w_gate_up': nrm((DEPTH, N_EXPERTS, D_MODEL, 2 * D_FF), D_MODEL ** -0.5),
        'b_gate_up': nrm((DEPTH, N_EXPERTS, 2 * D_FF), 0.01),
        'w_down': nrm((DEPTH, N_EXPERTS, D_FF, D_MODEL), D_FF ** -0.5),
        'b_down': nrm((DEPTH, N_EXPERTS, D_MODEL), 0.01),
    }


def reference(x_prompt, x_sample, cache_diff_k, cache_diff_v, cache_dsa_k, cache_dsa_v, cache_idx_k,
              page_table, c_prompt, c_sample, w_ada, b_ada, norm_attn_g, w_in, q_norm_diff_g,
              k_norm_diff_g, q_norm_dsa_g, k_norm_dsa_g, diff_lambda_qk, diff_subln_g, w_out,
              norm_ffn_g, w_router, b_router, w_gate_up, b_gate_up, w_down, b_down):
    slopes_diff = alibi_slopes(H_DIFF)
    slopes_dsa = alibi_slopes(H_DSA)
    xp, xs = x_prompt, x_sample
    new_p = [[] for _ in range(5)]
    new_s = [[] for _ in range(5)]
    for l in range(DEPTH):
        lam_init = lambda_init_for(l)
        lam = diff_lambda(diff_lambda_qk[l], lam_init)
        lw = (w_ada[l], b_ada[l], norm_attn_g[l], w_in[l], q_norm_diff_g[l], k_norm_diff_g[l],
              q_norm_dsa_g[l], k_norm_dsa_g[l], diff_subln_g[l], lam_init, w_out[l], norm_ffn_g[l],
              w_router[l], b_router[l], w_gate_up[l], b_gate_up[l], w_down[l], b_down[l])

        def attend_prompt(dq, dk, dv, sq, sk, sv, iq, ik, iw):
            return (diff_attn_prompt(dq, dk, dv, lam, slopes_diff),
                    dsa_prompt(sq, sk, sv, iq, ik, iw, slopes_dsa))

        def attend_sample(dq, dk, dv, sq, sk, sv, iq, ik, iw):
            return (diff_attn_sample(dq, dk, dv, cache_diff_k, cache_diff_v, l, page_table, lam, slopes_diff),
                    dsa_sample(sq, sk, sv, iq, ik, iw, cache_dsa_k, cache_dsa_v, cache_idx_k, l, page_table, slopes_dsa))

        xp, st_p = trunk_layer(xp, c_prompt, lw, attend_prompt)
        xs, st_s = trunk_layer(xs, c_sample, lw, attend_sample)
        for j in range(5):
            new_p[j].append(st_p[j])
            new_s[j].append(st_s[j])
    return (xp, xs,
            jnp.stack(new_p[0]), jnp.stack(new_p[1]), jnp.stack(new_p[2]), jnp.stack(new_p[3]), jnp.stack(new_p[4]),
            jnp.stack(new_s[0]), jnp.stack(new_s[1]), jnp.stack(new_s[2]), jnp.stack(new_s[3]), jnp.stack(new_s[4]))
```

```python
import functools
import math

import jax
import jax.numpy as jnp
from jax import lax
from jax.experimental import pallas as pl
from jax.experimental.pallas import tpu as pltpu

F32 = jnp.float32
BF16 = jnp.bfloat16
I32 = jnp.int32

D_HEAD = 64
H_DIFF = 4
H_DSA = 8
H_IDX = 8
D_IDX = 64
TOPK_KEYS = 256
TOP_K = 4
SWIGLU_LIMIT = 7.0
SWIGLU_ALPHA = 1.702
EPS = 1e-6

LANES = 128
SEG = 512
NEG = -1e30
INT_MIN = -2 ** 31
INT_MAX = 2 ** 31 - 1
VMEM_LIMIT = 56 * 1024 * 1024

OFF_IK = 7 * SEG
OFF_IW = OFF_IK + LANES
N_IN_PAD = OFF_IW + LANES


def _cparams(sem, vmem=VMEM_LIMIT):
    return pltpu.CompilerParams(dimension_semantics=sem, vmem_limit_bytes=vmem)


def _dot(a, b):
    return jnp.dot(a, b, preferred_element_type=F32)


def _dot_nt(a, b):
    return lax.dot_general(a, b, (((1,), (1,)), ((), ())), preferred_element_type=F32)


def _split_bf16(a):
    hi = a.astype(BF16)
    lo = (a - hi.astype(F32)).astype(BF16)
    return hi, lo


def _dot3(a, b):
    ah, al = _split_bf16(a)
    bh, bl = _split_bf16(b)
    return _dot(ah, bh) + _dot(al, bh) + _dot(ah, bl)


def _rms(x):
    return x * lax.rsqrt(jnp.mean(x * x, axis=-1, keepdims=True) + EPS)


def _ada_kernel(c_ref, w_ref, b_ref, o_ref):
    c = c_ref[...]
    o_ref[...] = _dot3(c * jax.nn.sigmoid(c), w_ref[...]) + b_ref[...]


def _ada(c, w, b):
    rows, d = c.shape
    n = w.shape[1]
    tn = 1024 if n % 1024 == 0 else n
    return pl.pallas_call(
        _ada_kernel,
        out_shape=jax.ShapeDtypeStruct((rows, n), F32),
        grid=(n // tn,),
        in_specs=[pl.BlockSpec((rows, d), lambda j: (0, 0)),
                  pl.BlockSpec((d, tn), lambda j: (0, j)),
                  pl.BlockSpec((1, tn), lambda j: (0, j))],
        out_specs=pl.BlockSpec((rows, tn), lambda j: (0, j)),
        compiler_params=_cparams(("arbitrary",)),
        name="ada",
    )(c, w, b.reshape(1, n))


def _inproj_kernel(x_ref, sc_ref, sh_ref, g_ref, w_ref, bd_ref, gains_ref,
                   dq_ref, dk_ref, dkb_ref, dv_ref, dvb_ref,
                   sq_ref, sk_ref, skb_ref, sv_ref, svb_ref,
                   iq_ref, ik_ref, ikb_ref, iw_ref):
    x = x_ref[0]
    h = _rms(x) * g_ref[...] * (1.0 + sc_ref[0]) + sh_ref[0]
    hb = h.astype(BF16)

    def seg(i):
        return _dot(hb, w_ref[:, i * SEG:(i + 1) * SEG])

    def head_norm(y, row):
        ms = _dot((y * y).astype(BF16), bd_ref[...])
        return y * lax.rsqrt(ms + EPS) * gains_ref[row:row + 1, :]

    scale = D_HEAD ** -0.5
    dq_ref[0] = (head_norm(seg(0), 0) * scale).astype(BF16)
    dk = head_norm(seg(1), 1)
    dk_ref[0] = dk
    dkb_ref[0] = dk.astype(BF16)
    dv = seg(2)
    dv_ref[0] = dv
    dvb_ref[0] = dv.astype(BF16)
    sq_ref[0] = (head_norm(seg(3), 2) * scale).astype(BF16)
    sk = head_norm(seg(4), 3)
    sk_ref[0] = sk
    skb_ref[0] = sk.astype(BF16)
    sv = seg(5)
    sv_ref[0] = sv
    svb_ref[0] = sv.astype(BF16)
    iq_ref[0] = seg(6).astype(BF16)
    ik = _dot(hb, w_ref[:, OFF_IK:OFF_IK + LANES])[:, :D_IDX]
    ik_ref[0] = ik
    ikb_ref[0] = ik.astype(BF16)
    iw_ref[0] = _dot(hb, w_ref[:, OFF_IW:OFF_IW + LANES])[:, :H_IDX] * (H_IDX ** -0.5)


def _inproj(x, sc, sh, g, w_pad, bd, gains, tm):
    B, S, D = x.shape
    mod_rows = sc.shape[1]
    mod_blk = (1, 1, D) if mod_rows == 1 else (1, tm, D)
    mod_map = (lambda b, i: (b, 0, 0)) if mod_rows == 1 else (lambda b, i: (b, i, 0))
    row = lambda w: pl.BlockSpec((1, tm, w), lambda b, i: (b, i, 0))
    const = lambda shp: pl.BlockSpec(shp, lambda b, i: (0,) * len(shp))
    sds = lambda w, dt: jax.ShapeDtypeStruct((B, S, w), dt)
    outs = [(SEG, BF16), (SEG, F32), (SEG, BF16), (SEG, F32), (SEG, BF16),
            (SEG, BF16), (SEG, F32), (SEG, BF16), (SEG, F32), (SEG, BF16),
            (SEG, BF16), (D_IDX, F32), (D_IDX, BF16), (H_IDX, F32)]
    return pl.pallas_call(
        _inproj_kernel,
        out_shape=[sds(w, dt) for w, dt in outs],
        grid=(B, S // tm),
        in_specs=[row(D), pl.BlockSpec(mod_blk, mod_map), pl.BlockSpec(mod_blk, mod_map),
                  const((1, D)), const((D, N_IN_PAD)), const((SEG, SEG)), const((4, SEG))],
        out_specs=[row(w) for w, _ in outs],
        compiler_params=_cparams(("parallel", "parallel")),
        name="inproj",
    )(x, sc, sh, g, w_pad, bd, gains)


def _diff_prompt_kernel(slopes_ref, q_ref, k_ref, v_ref, lam_ref, subg_ref, o_ref,
                        m_sc, l_sc, acc_sc, *, tq, tk, lam_init):
    h = pl.program_id(1)
    q0 = pl.program_id(2) * tq
    slope = slopes_ref[h]
    q = q_ref[0]
    lane = lax.broadcasted_iota(I32, (1, LANES), 1)
    qs = (jnp.where(lane < D_HEAD, q, jnp.zeros_like(q)),
          jnp.where(lane >= D_HEAD, q, jnp.zeros_like(q)))
    rowpos = q0 + lax.broadcasted_iota(I32, (tq, 1), 0)
    col = lax.broadcasted_iota(I32, (1, tk), 1)

    m_sc[...] = jnp.full(m_sc.shape, NEG, F32)
    l_sc[...] = jnp.zeros(l_sc.shape, F32)
    acc_sc[...] = jnp.zeros(acc_sc.shape, F32)

    def chunk(c, masked):
        start = pl.multiple_of(c * tk, tk)
        k = k_ref[0, pl.ds(start, tk), :]
        v = v_ref[0, pl.ds(start, tk), :]
        colpos = start + col
        bias = slope * (colpos - q0).astype(F32)
        for j in range(2):
            s = _dot_nt(qs[j], k) + bias
            if masked:
                s = jnp.where(colpos <= rowpos, s, NEG)
            m_old = m_sc[j]
            m_new = jnp.maximum(m_old, jnp.max(s, axis=-1, keepdims=True))
            alpha = jnp.exp(m_old - m_new)
            p = jnp.exp(s - m_new)
            l_sc[j] = alpha * l_sc[j] + jnp.sum(p, axis=-1, keepdims=True)
            acc_sc[j] = alpha * acc_sc[j] + _dot(p.astype(BF16), v)
            m_sc[j] = m_new

    n_full = q0 // tk
    n_all = (q0 + tq + tk - 1) // tk

    def full_body(c, carry):
        chunk(c, False)
        return carry

    def diag_body(c, carry):
        chunk(c, True)
        return carry

    lax.fori_loop(0, n_full, full_body, 0)
    lax.fori_loop(n_full, n_all, diag_body, 0)

    lp = lam_ref[...]
    lam = (jnp.exp(jnp.sum(lp[0:1] * lp[1:2], axis=-1, keepdims=True))
           - jnp.exp(jnp.sum(lp[2:3] * lp[3:4], axis=-1, keepdims=True)) + lam_init)
    o = acc_sc[0] / l_sc[0] - lam * (acc_sc[1] / l_sc[1])
    o_ref[0] = (_rms(o) * subg_ref[...] * (1.0 - lam_init)).astype(BF16)


def _diff_prompt(slopes, dq, dkb, dvb, lam_p, subg, lam_init, tq, tk):
    B, S, _ = dq.shape
    kern = functools.partial(_diff_prompt_kernel, tq=tq, tk=tk, lam_init=lam_init)
    return pl.pallas_call(
        kern,
        out_shape=jax.ShapeDtypeStruct((B, S, SEG), BF16),
        grid=(B, H_DIFF, S // tq),
        in_specs=[pl.BlockSpec(memory_space=pltpu.SMEM),
                  pl.BlockSpec((1, tq, LANES), lambda b, h, i: (b, i, h)),
                  pl.BlockSpec((1, S, LANES), lambda b, h, i: (b, 0, h)),
                  pl.BlockSpec((1, S, LANES), lambda b, h, i: (b, 0, h)),
                  pl.BlockSpec((4, D_HEAD), lambda b, h, i: (0, 0)),
                  pl.BlockSpec((1, LANES), lambda b, h, i: (0, 0))],
        out_specs=pl.BlockSpec((1, tq, LANES), lambda b, h, i: (b, i, h)),
        scratch_shapes=[pltpu.VMEM((2, tq, 1), F32), pltpu.VMEM((2, tq, 1), F32),
                        pltpu.VMEM((2, tq, LANES), F32)],
        compiler_params=_cparams(("parallel", "parallel", "parallel")),
        name="diff_prompt",
    )(slopes, dq, dkb, dvb, lam_p, subg)


def _sortable(x):
    x = jnp.where(x == 0.0, jnp.zeros_like(x), x)
    bits = pltpu.bitcast(x, I32)
    return bits ^ ((bits >> 31) & INT_MAX)


def _count(key_sc, n_chunks, tk, rows, pred, extra=None):
    def body(c, acc):
        ind = jnp.where(pred(key_sc[c], c), 1.0, 0.0)
        part = ind[:, 0:LANES]
        for j in range(1, tk // LANES):
            part = part + ind[:, j * LANES:(j + 1) * LANES]
        return acc + part

    acc = lax.fori_loop(0, n_chunks, body, jnp.zeros((rows, LANES), F32))
    cnt = jnp.sum(acc, axis=-1, keepdims=True)
    if extra is not None:
        cnt = cnt + jnp.where(extra, 1.0, 0.0)
    return cnt


def _topk_select(key_sc, n_chunks, tk, rows, k_sel, idx_bits, cut_sc, key_new=None, idx_new=None):
    col = lax.broadcasted_iota(I32, (1, tk), 1)

    def count(pred, pred_new=None):
        extra = None if key_new is None else pred_new
        return _count(key_sc, n_chunks, tk, rows, pred, extra)

    def count_ge(cand):
        return count(lambda kc, c: kc >= cand, None if key_new is None else key_new >= cand)

    kf = float(k_sel)
    thr = jnp.where(count_ge(jnp.zeros((rows, 1), I32)) >= kf, 0, INT_MIN).astype(I32)

    def bit_body(i, thr):
        cand = thr | lax.shift_left(jnp.int32(1), 30 - i)
        return jnp.where(count_ge(cand) >= kf, cand, thr)

    thr = lax.fori_loop(0, 31, bit_body, thr)

    cnt_gt = count(lambda kc, c: kc > thr, None if key_new is None else key_new > thr)
    cnt_ge = count_ge(thr)
    need = kf - cnt_gt
    tie = (cnt_ge - cnt_gt) > need
    cut_sc[...] = jnp.full((rows, 1), INT_MAX, I32)

    @pl.when(jnp.max(jnp.where(tie, 1.0, 0.0)) > 0.0)
    def _():
        def idx_body(i, cut):
            cand = cut | lax.shift_left(jnp.int32(1), idx_bits - 1 - i)
            cnt = count(lambda kc, c: (kc == thr) & (c * tk + col < cand),
                        None if key_new is None else (key_new == thr) & (idx_new < cand))
            return jnp.where(cnt < need, cand, cut)

        cut = lax.fori_loop(0, idx_bits, idx_body, jnp.zeros((rows, 1), I32))
        cut_sc[...] = jnp.where(tie, cut, INT_MAX)

    return thr, cut_sc[...]


def _dsa_prompt_kernel(slopes_ref, q_ref, iq_ref, iw_ref, k_ref, v_ref, ik_ref, o_ref,
                       key_sc, bias_sc, cut_sc, m_sc, l_sc, acc_sc, *, tq, tk, k_sel, idx_bits):
    q0 = pl.program_id(1) * tq
    n_chunks = (q0 + tq + tk - 1) // tk
    rowpos = q0 + lax.broadcasted_iota(I32, (tq, 1), 0)
    col = lax.broadcasted_iota(I32, (1, tk), 1)

    def idx_body(c, carry):
        start = pl.multiple_of(c * tk, tk)
        ikc = ik_ref[0, pl.ds(start, tk), :]
        acc = jnp.zeros((tq, tk), F32)
        for h in range(H_IDX):
            r = jnp.maximum(_dot_nt(iq_ref[0, :, h * D_IDX:(h + 1) * D_IDX], ikc), 0.0)
            acc = acc + r * iw_ref[0, :, h:h + 1]
        acc = jnp.where(start + col <= rowpos, acc, -jnp.inf)
        key_sc[c] = _sortable(acc)
        return carry

    lax.fori_loop(0, n_chunks, idx_body, 0)

    thr, cut = _topk_select(key_sc, n_chunks, tk, tq, k_sel, idx_bits, cut_sc)

    def mask_body(c, carry):
        kc = key_sc[c]
        idx = c * tk + col
        sel = (kc > thr) | ((kc == thr) & (idx <= cut))
        bias_sc[c] = jnp.where(sel & (idx <= rowpos), 0.0, NEG)
        return carry

    lax.fori_loop(0, n_chunks, mask_body, 0)

    m_sc[...] = jnp.full(m_sc.shape, NEG, F32)
    l_sc[...] = jnp.zeros(l_sc.shape, F32)
    acc_sc[...] = jnp.zeros(acc_sc.shape, F32)
    lane = lax.broadcasted_iota(I32, (1, LANES), 1)
    is_lo = lane < D_HEAD

    def att_body(c, carry):
        start = pl.multiple_of(c * tk, tk)
        mb = bias_sc[c]
        rel = (start - q0 + col).astype(F32)
        for p in range(H_DSA // 2):
            qp = q_ref[0, :, p * LANES:(p + 1) * LANES]
            kp = k_ref[0, pl.ds(start, tk), p * LANES:(p + 1) * LANES]
            vp = v_ref[0, pl.ds(start, tk), p * LANES:(p + 1) * LANES]
            pv, alphas = [], []
            for half in range(2):
                h = 2 * p + half
                qh = jnp.where(is_lo if half == 0 else ~is_lo, qp, jnp.zeros_like(qp))
                s = _dot_nt(qh, kp) + (mb + slopes_ref[h] * rel)
                m_old = m_sc[h]
                m_new = jnp.maximum(m_old, jnp.max(s, axis=-1, keepdims=True))
                alpha = jnp.exp(m_old - m_new)
                pr = jnp.exp(s - m_new)
                l_sc[h] = alpha * l_sc[h] + jnp.sum(pr, axis=-1, keepdims=True)
                m_sc[h] = m_new
                pv.append(_dot(pr.astype(BF16), vp))
                alphas.append(alpha)
            acc_sc[p] = (acc_sc[p] * jnp.where(is_lo, alphas[0], alphas[1])
                         + jnp.where(is_lo, pv[0], pv[1]))
        return carry

    lax.fori_loop(0, n_chunks, att_body, 0)

    for p in range(H_DSA // 2):
        o_ref[0, :, p * LANES:(p + 1) * LANES] = (
            acc_sc[p] / jnp.where(is_lo, l_sc[2 * p], l_sc[2 * p + 1])).astype(BF16)


def _dsa_prompt(slopes, sq, iq, iw, skb, svb, ikb, tq, tk):
    B, S, _ = sq.shape
    k_sel = min(TOPK_KEYS, S // 4)
    kern = functools.partial(_dsa_prompt_kernel, tq=tq, tk=tk, k_sel=k_sel,
                             idx_bits=max(1, (S - 1).bit_length()))
    nck = S // tk
    return pl.pallas_call(
        kern,
        out_shape=jax.ShapeDtypeStruct((B, S, SEG), BF16),
        grid=(B, S // tq),
        in_specs=[pl.BlockSpec(memory_space=pltpu.SMEM),
                  pl.BlockSpec((1, tq, SEG), lambda b, i: (b, i, 0)),
                  pl.BlockSpec((1, tq, SEG), lambda b, i: (b, i, 0)),
                  pl.BlockSpec((1, tq, H_IDX), lambda b, i: (b, i, 0)),
                  pl.BlockSpec((1, S, SEG), lambda b, i: (b, 0, 0)),
                  pl.BlockSpec((1, S, SEG), lambda b, i: (b, 0, 0)),
                  pl.BlockSpec((1, S, D_IDX), lambda b, i: (b, 0, 0))],
        out_specs=pl.BlockSpec((1, tq, SEG), lambda b, i: (b, i, 0)),
        scratch_shapes=[pltpu.VMEM((nck, tq, tk), I32), pltpu.VMEM((nck, tq, tk), F32),
                        pltpu.VMEM((tq, 1), I32),
                        pltpu.VMEM((H_DSA, tq, 1), F32), pltpu.VMEM((H_DSA, tq, 1), F32),
                        pltpu.VMEM((H_DSA // 2, tq, LANES), F32)],
        compiler_params=_cparams(("parallel", "parallel")),
        name="dsa_prompt",
    )(slopes, sq, iq, iw, skb, svb, ikb)


def _sample_index_kernel(pt_ref, iq_ref, iw_ref, iknew_ref, *refs, n_pages):
    page_refs = refs[:n_pages]
    o_ref, onew_ref = refs[n_pages:]
    iq = iq_ref[0]
    iw = iw_ref[0]
    for i in range(n_pages):
        r = jnp.maximum(_dot(iq, page_refs[i][0].astype(BF16)), 0.0)
        o_ref[0, :, i * LANES:(i + 1) * LANES] = jnp.sum(r * iw, axis=0, keepdims=True)

    @pl.when(pl.program_id(1) == 0)
    def _():
        kn = iknew_ref[0].astype(BF16).astype(F32)
        rn = jnp.maximum(jnp.sum(iq.astype(F32) * kn, axis=-1, keepdims=True), 0.0)
        onew_ref[0] = jnp.sum(rn * iw, axis=0, keepdims=True)


def _sample_index(pt_flat, iq, iw, ik_new, pool, n_pg, n_pages, page_base):
    DB = iq.shape[0]
    page = pool.shape[2]
    assert page == LANES
    kern = functools.partial(_sample_index_kernel, n_pages=n_pages)

    def page_spec(i):
        return pl.BlockSpec((1, D_IDX, page),
                            lambda b, g, pt: (page_base + pt[b * n_pg + g * n_pages + i], 0, 0))

    grid_spec = pltpu.PrefetchScalarGridSpec(
        num_scalar_prefetch=1,
        grid=(DB, n_pg // n_pages),
        in_specs=[pl.BlockSpec((1, H_IDX, D_IDX), lambda b, g, pt: (b, 0, 0)),
                  pl.BlockSpec((1, H_IDX, 1), lambda b, g, pt: (b, 0, 0)),
                  pl.BlockSpec((1, 1, D_IDX), lambda b, g, pt: (b, 0, 0))]
                 + [page_spec(i) for i in range(n_pages)],
        out_specs=[pl.BlockSpec((1, 1, n_pages * page), lambda b, g, pt: (b, 0, g)),
                   pl.BlockSpec((1, 1, 1), lambda b, g, pt: (b, 0, 0))],
    )
    return pl.pallas_call(
        kern,
        out_shape=[jax.ShapeDtypeStruct((DB, 1, n_pg * page), F32),
                   jax.ShapeDtypeStruct((DB, 1, 1), F32)],
        grid_spec=grid_spec,
        compiler_params=_cparams(("parallel", "arbitrary")),
        name="sample_index",
    )(pt_flat, iq, iw, ik_new, *([pool] * n_pages))


def _sample_select_kernel(s_ref, snew_ref, bias_ref, newok_ref, key_sc, cut_sc,
                          *, n_chunks, tk, rows, k_sel, idx_bits):
    def to_keys(c, carry):
        key_sc[c] = _sortable(s_ref[c])
        return carry

    lax.fori_loop(0, n_chunks, to_keys, 0)
    key_new = _sortable(snew_ref[...])
    idx_new = jnp.full((rows, 1), n_chunks * tk, I32)
    thr, cut = _topk_select(key_sc, n_chunks, tk, rows, k_sel, idx_bits, cut_sc,
                            key_new=key_new, idx_new=idx_new)
    col = lax.broadcasted_iota(I32, (1, tk), 1)

    def mask_body(c, carry):
        kc = key_sc[c]
        sel = (kc > thr) | ((kc == thr) & (c * tk + col <= cut))
        bias_ref[c] = jnp.where(sel, 0.0, NEG)
        return carry

    lax.fori_loop(0, n_chunks, mask_body, 0)
    sel_new = (key_new > thr) | ((key_new == thr) & (idx_new <= cut))
    newok_ref[...] = jnp.where(sel_new, 1.0, 0.0)


def _sample_select(scores, score_new, tk):
    DB, P = scores.shape
    n_chunks = P // tk
    k_sel = min(TOPK_KEYS, (P + 1) // 4)
    s3 = scores.reshape(DB, n_chunks, tk).transpose(1, 0, 2)
    kern = functools.partial(_sample_select_kernel, n_chunks=n_chunks, tk=tk, rows=DB, k_sel=k_sel,
                             idx_bits=max(1, P.bit_length()))
    bias3, newok = pl.pallas_call(
        kern,
        out_shape=[jax.ShapeDtypeStruct((n_chunks, DB, tk), F32), jax.ShapeDtypeStruct((DB, 1), F32)],
        scratch_shapes=[pltpu.VMEM((n_chunks, DB, tk), I32), pltpu.VMEM((DB, 1), I32)],
        compiler_params=pltpu.CompilerParams(vmem_limit_bytes=VMEM_LIMIT),
        name="sample_select",
    )(s3, score_new)
    return bias3.transpose(1, 0, 2).reshape(DB, P), newok


def _paged_attn_kernel(pt_ref, q_ref, slope_ref, knew_ref, vnew_ref, newok_ref, bias_ref, *refs,
                       n_pages, n_rows, past_len, v_transposed, epilogue):
    k_refs = refs[:n_pages]
    v_refs = refs[n_pages:2 * n_pages]
    extra_refs = refs[2 * n_pages:-4]
    o_ref, m_sc, l_sc, acc_sc = refs[-4:]
    g = pl.program_id(1)
    q = q_ref[0]
    page = LANES

    @pl.when(g == 0)
    def _():
        kn = knew_ref[0].astype(BF16).astype(F32)
        ok = newok_ref[0] > 0.5
        s_new = jnp.sum(q.astype(F32) * kn, axis=-1, keepdims=True)
        m_sc[...] = jnp.where(ok, s_new, NEG)
        l_sc[...] = jnp.where(ok, jnp.ones((n_rows, 1), F32), 0.0)
        vn = vnew_ref[0].astype(BF16).astype(F32)
        acc_sc[...] = jnp.where(ok, jnp.broadcast_to(vn, (n_rows, SEG)), 0.0)

    col = lax.broadcasted_iota(I32, (1, page), 1)
    for i in range(n_pages):
        kt = k_refs[i][0].astype(BF16)
        pos = (g * n_pages + i) * page + col
        rel = (pos - past_len).astype(F32)
        s = _dot(q, kt) + slope_ref[...] * rel + bias_ref[0, :, i * page:(i + 1) * page]
        m_old = m_sc[...]
        m_new = jnp.maximum(m_old, jnp.max(s, axis=-1, keepdims=True))
        alpha = jnp.exp(m_old - m_new)
        p = jnp.exp(s - m_new)
        l_sc[...] = alpha * l_sc[...] + jnp.sum(p, axis=-1, keepdims=True)
        p = p.astype(BF16)
        if v_transposed:
            pv = _dot_nt(p, v_refs[i][0].astype(BF16))
        else:
            n_h = v_refs[i].shape[2]
            pv = jnp.concatenate([_dot(p, v_refs[i][0, :, h, :].astype(BF16)) for h in range(n_h)], axis=-1)
        acc_sc[...] = alpha * acc_sc[...] + pv
        m_sc[...] = m_new

    @pl.when(g == pl.num_programs(1) - 1)
    def _():
        o_ref[0] = epilogue(acc_sc[...] / l_sc[...], *extra_refs)


def _diff_sample_epilogue(on, lam_ref, subg_ref, *, lam_init):
    row = lax.broadcasted_iota(I32, on.shape, 0)
    colh = lax.broadcasted_iota(I32, on.shape, 1) // (2 * D_HEAD)
    own = colh == row // 2
    o1 = jnp.sum(jnp.where(own & (row % 2 == 0), on, 0.0), axis=0, keepdims=True)
    o2 = jnp.sum(jnp.where(own & (row % 2 == 1), on, 0.0), axis=0, keepdims=True)
    lp = lam_ref[...]
    lam = (jnp.exp(jnp.sum(lp[0:1] * lp[1:2], axis=-1, keepdims=True))
           - jnp.exp(jnp.sum(lp[2:3] * lp[3:4], axis=-1, keepdims=True)) + lam_init)
    o = o1 - lam * o2
    w = 2 * D_HEAD
    segs = [_rms(o[:, h * w:(h + 1) * w]) * subg_ref[...] * (1.0 - lam_init) for h in range(H_DIFF)]
    return jnp.concatenate(segs, axis=-1)


def _dsa_sample_epilogue(on):
    row = lax.broadcasted_iota(I32, on.shape, 0)
    colh = lax.broadcasted_iota(I32, on.shape, 1) // D_HEAD
    return jnp.sum(jnp.where(colh == row, on, 0.0), axis=0, keepdims=True)


def _paged_attn(pt_flat, qmat, row_slope, k_new, v_new, newok, bias, k_pool, v_pool, extras,
                epilogue, n_pg, n_pages, page_base, name):
    DB, R, _ = qmat.shape
    page = k_pool.shape[2]
    assert page == LANES
    v_transposed = v_pool.ndim == 3
    kern = functools.partial(_paged_attn_kernel, n_pages=n_pages, n_rows=R, past_len=n_pg * page,
                             v_transposed=v_transposed, epilogue=epilogue)

    def page_spec(i, pool):
        blk = (1,) + pool.shape[1:]
        zeros = (0,) * (pool.ndim - 1)
        return pl.BlockSpec(blk, lambda b, g, pt: (page_base + pt[b * n_pg + g * n_pages + i],) + zeros)

    per_b = lambda shp: pl.BlockSpec(shp, lambda b, g, pt: (b, 0, 0))
    grid_spec = pltpu.PrefetchScalarGridSpec(
        num_scalar_prefetch=1,
        grid=(DB, n_pg // n_pages),
        in_specs=[per_b((1, R, SEG)), pl.BlockSpec((R, 1), lambda b, g, pt: (0, 0)),
                  per_b((1, 1, SEG)), per_b((1, 1, SEG)), per_b((1, 1, 1)),
                  pl.BlockSpec((1, 1, n_pages * page), lambda b, g, pt: (b, 0, g))]
                 + [page_spec(i, k_pool) for i in range(n_pages)]
                 + [page_spec(i, v_pool) for i in range(n_pages)]
                 + [pl.BlockSpec(e.shape, lambda b, g, pt: (0, 0)) for e in extras],
        out_specs=per_b((1, 1, SEG)),
        scratch_shapes=[pltpu.VMEM((R, 1), F32), pltpu.VMEM((R, 1), F32), pltpu.VMEM((R, SEG), F32)],
    )
    return pl.pallas_call(
        kern,
        out_shape=jax.ShapeDtypeStruct((DB, 1, SEG), F32),
        grid_spec=grid_spec,
        compiler_params=_cparams(("parallel", "arbitrary")),
        name=name,
    )(pt_flat, qmat, row_slope, k_new, v_new, newok, bias,
      *([k_pool] * n_pages), *([v_pool] * n_pages), *extras)


def _block_rows(q, width):
    n = SEG // width
    own = (jnp.arange(SEG)[None, :] // width) == jnp.arange(n)[:, None]
    return jnp.where(own[None], q[:, None, :], jnp.zeros((), q.dtype))


def _outproj_kernel(od_ref, os_ref, x_ref, ga_ref, scf_ref, shf_ref, g_ref, wo_ref, wrh_ref, wrl_ref, br_ref,
                    x1_ref, h2_ref, e_ref, gate_ref, *, n_experts):
    o = _dot(od_ref[0], wo_ref[0:SEG, :]) + _dot(os_ref[0], wo_ref[SEG:2 * SEG, :])
    x1 = x_ref[0] + ga_ref[0] * o
    x1_ref[0] = x1
    h2 = _rms(x1) * g_ref[...] * (1.0 + scf_ref[0]) + shf_ref[0]
    h2_ref[0] = h2
    hh, hl = _split_bf16(h2)
    logits = _dot(hh, wrh_ref[...]) + _dot(hl, wrh_ref[...]) + _dot(hh, wrl_ref[...]) + br_ref[...]
    lane = lax.broadcasted_iota(I32, logits.shape, 1).astype(F32)
    lg = jnp.where(lane < n_experts, logits, -jnp.inf)
    vals, idxs = [], []
    for _ in range(TOP_K):
        m = jnp.max(lg, axis=-1, keepdims=True)
        idx = jnp.min(jnp.where(lg == m, lane, float(LANES)), axis=-1, keepdims=True)
        lg = jnp.where(lane == idx, -jnp.inf, lg)
        vals.append(m)
        idxs.append(idx)
    ex = [jnp.exp(v - vals[0]) for v in vals]
    tot = ex[0] + ex[1] + ex[2] + ex[3]
    e_ref[0] = jnp.concatenate(idxs, axis=-1).astype(I32)
    gate_ref[0] = jnp.concatenate([e / tot for e in ex], axis=-1)


def _outproj(od, osx, x, ga, scf, shf, g, wo, wr_hi, wr_lo, br, n_experts, tm):
    B, S, D = x.shape
    mod_rows = ga.shape[1]
    mod_blk = (1, 1, D) if mod_rows == 1 else (1, tm, D)
    mod_map = (lambda b, i: (b, 0, 0)) if mod_rows == 1 else (lambda b, i: (b, i, 0))
    row = lambda w: pl.BlockSpec((1, tm, w), lambda b, i: (b, i, 0))
    const = lambda shp: pl.BlockSpec(shp, lambda b, i: (0,) * len(shp))
    kern = functools.partial(_outproj_kernel, n_experts=n_experts)
    return pl.pallas_call(
        kern,
        out_shape=[jax.ShapeDtypeStruct((B, S, D), F32), jax.ShapeDtypeStruct((B, S, D), F32),
                   jax.ShapeDtypeStruct((B, S, TOP_K), I32), jax.ShapeDtypeStruct((B, S, TOP_K), F32)],
        grid=(B, S // tm),
        in_specs=[row(SEG), row(SEG), row(D)] + [pl.BlockSpec(mod_blk, mod_map)] * 3
                 + [const((1, D)), const((2 * SEG, D)), const((D, LANES)), const((D, LANES)), const((1, LANES))],
        out_specs=[row(D), row(D), row(TOP_K), row(TOP_K)],
        compiler_params=_cparams(("parallel", "parallel")),
        name="outproj_router",
    )(od, osx, x, ga, scf, shf, g, wo, wr_hi, wr_lo, br)


def _rank_kernel(e_ref, rank_ref, cnt_ref, carry_sc, *, tr):
    @pl.when(pl.program_id(0) == 0)
    def _():
        carry_sc[...] = jnp.zeros(carry_sc.shape, F32)

    e = e_ref[...]
    lane = lax.broadcasted_iota(I32, (tr, LANES), 1)
    ohs = [lane == e[:, k:k + 1] for k in range(TOP_K)]
    oh = jnp.zeros((tr, LANES), F32)
    for m in ohs:
        oh = oh + jnp.where(m, 1.0, 0.0)
    r = lax.broadcasted_iota(I32, (tr, tr), 0)
    c = lax.broadcasted_iota(I32, (tr, tr), 1)
    lower = jnp.where(r > c, 1.0, 0.0).astype(BF16)
    rank_full = carry_sc[...] + _dot(lower, oh.astype(BF16))
    cols = [jnp.sum(jnp.where(m, rank_full, 0.0), axis=-1, keepdims=True) for m in ohs]
    rank_ref[...] = jnp.concatenate(cols, axis=-1).astype(I32)
    carry_sc[...] = carry_sc[...] + jnp.sum(oh, axis=0, keepdims=True)
    cnt_ref[...] = carry_sc[...]


def _ranks(e_idx, tr):
    T = e_idx.shape[0]
    kern = functools.partial(_rank_kernel, tr=tr)
    return pl.pallas_call(
        kern,
        out_shape=[jax.ShapeDtypeStruct((T, TOP_K), I32), jax.ShapeDtypeStruct((1, LANES), F32)],
        grid=(T // tr,),
        in_specs=[pl.BlockSpec((tr, TOP_K), lambda i: (i, 0))],
        out_specs=[pl.BlockSpec((tr, TOP_K), lambda i: (i, 0)), pl.BlockSpec((1, LANES), lambda i: (0, 0))],
        scratch_shapes=[pltpu.VMEM((1, LANES), F32)],
        compiler_params=_cparams(("arbitrary",)),
        name="moe_rank",
    )(e_idx)


def _row_copy(src_ref, src_row, dst_ref, dst_row, sem):
    return pltpu.make_async_copy(src_ref.at[pl.ds(src_row, 1)], dst_ref.at[pl.ds(dst_row, 1)], sem)


def _scatter_kernel(dest_ref, h_ref, xb_in_ref, xb_ref, sem, *, ts):
    del xb_in_ref

    def start(r, carry):
        for k in range(TOP_K):
            _row_copy(h_ref, r, xb_ref, dest_ref[r * TOP_K + k], sem).start()
        return carry

    def wait(r, carry):
        for k in range(TOP_K):
            _row_copy(h_ref, r, xb_ref, dest_ref[r * TOP_K + k], sem).wait()
        return carry

    lax.fori_loop(0, ts, start, 0)
    lax.fori_loop(0, ts, wait, 0)


def _scatter(dest_flat, h2, xb, ts):
    T, D = h2.shape
    kern = functools.partial(_scatter_kernel, ts=ts)
    return pl.pallas_call(
        kern,
        out_shape=jax.ShapeDtypeStruct(xb.shape, xb.dtype),
        grid=(T // ts,),
        in_specs=[pl.BlockSpec((ts * TOP_K,), lambda i: (i,), memory_space=pltpu.SMEM),
                  pl.BlockSpec((ts, D), lambda i: (i, 0)),
                  pl.BlockSpec(memory_space=pl.ANY)],
        out_specs=pl.BlockSpec(memory_space=pl.ANY),
        scratch_shapes=[pltpu.SemaphoreType.DMA(())],
        input_output_aliases={2: 0},
        compiler_params=pltpu.CompilerParams(dimension_semantics=("arbitrary",), vmem_limit_bytes=VMEM_LIMIT,
                                             has_side_effects=True),
        name="moe_scatter",
    )(dest_flat, h2, xb)


def _expert_kernel(blk_e_ref, n_used_ref, x_ref, wgu_ref, bgu_ref, wd_ref, bd_ref, y_ref, *, d_ff, chunk):
    del blk_e_ref

    @pl.when(pl.program_id(0) < n_used_ref[0])
    def _():
        x = x_ref[...].astype(BF16)
        acc = jnp.broadcast_to(bd_ref[0], y_ref.shape)
        for c in range(d_ff // chunk):
            lo = c * chunk
            g = _dot(x, wgu_ref[0, :, lo:lo + chunk]) + bgu_ref[0, :, lo:lo + chunk]
            u = _dot(x, wgu_ref[0, :, d_ff + lo:d_ff + lo + chunk]) + bgu_ref[0, :, d_ff + lo:d_ff + lo + chunk]
            g = jnp.minimum(g, SWIGLU_LIMIT)
            u = jnp.clip(u, -SWIGLU_LIMIT, SWIGLU_LIMIT)
            a = (u + 1.0) * (g * jax.nn.sigmoid(SWIGLU_ALPHA * g))
            acc = acc + _dot(a.astype(BF16), wd_ref[0, lo:lo + chunk, :])
        y_ref[...] = acc


def _experts(blk_e, n_used, xb, wgu, bgu, wd, bd, blk):
    n_rows, D = xb.shape
    E, _, two_ff = wgu.shape
    d_ff = two_ff // 2
    chunk = 512 if d_ff % 512 == 0 else d_ff
    kern = functools.partial(_expert_kernel, d_ff=d_ff, chunk=chunk)
    grid_spec = pltpu.PrefetchScalarGridSpec(
        num_scalar_prefetch=2,
        grid=(n_rows // blk,),
        in_specs=[pl.BlockSpec((blk, D), lambda j, be, nu: (j, 0)),
                  pl.BlockSpec((1, D, two_ff), lambda j, be, nu: (be[j], 0, 0)),
                  pl.BlockSpec((1, 1, two_ff), lambda j, be, nu: (be[j], 0, 0)),
                  pl.BlockSpec((1, d_ff, D), lambda j, be, nu: (be[j], 0, 0)),
                  pl.BlockSpec((1, 1, D), lambda j, be, nu: (be[j], 0, 0))],
        out_specs=pl.BlockSpec((blk, D), lambda j, be, nu: (j, 0)),
    )
    return pl.pallas_call(
        kern,
        out_shape=jax.ShapeDtypeStruct((n_rows, D), F32),
        grid_spec=grid_spec,
        compiler_params=_cparams(("arbitrary",)),
        name="moe_experts",
    )(blk_e, n_used, xb, wgu, bgu, wd, bd)


def _combine_kernel(dest_ref, yb_ref, gate_ref, x1_ref, gf_ref, o_ref, buf, sem, *, tc):
    def start(r, carry):
        for k in range(TOP_K):
            _row_copy(yb_ref, dest_ref[r * TOP_K + k], buf.at[k], r, sem).start()
        return carry

    def wait(r, carry):
        for k in range(TOP_K):
            _row_copy(yb_ref, dest_ref[r * TOP_K + k], buf.at[k], r, sem).wait()
        return carry

    lax.fori_loop(0, tc, start, 0)
    lax.fori_loop(0, tc, wait, 0)
    gate = gate_ref[0]
    y = gate[:, 0:1] * buf[0]
    for k in range(1, TOP_K):
        y = y + gate[:, k:k + 1] * buf[k]
    o_ref[0] = x1_ref[0] + gf_ref[0] * y


def _combine(dest_flat, yb, gate, x1, gf, tc):
    B, S, D = x1.shape
    nt = S // tc
    mod_rows = gf.shape[1]
    mod_blk = (1, 1, D) if mod_rows == 1 else (1, tc, D)
    mod_map = (lambda b, i: (b, 0, 0)) if mod_rows == 1 else (lambda b, i: (b, i, 0))
    kern = functools.partial(_combine_kernel, tc=tc)
    return pl.pallas_call(
        kern,
        out_shape=jax.ShapeDtypeStruct((B, S, D), F32),
        grid=(B, nt),
        in_specs=[pl.BlockSpec((tc * TOP_K,), lambda b, i: (b * nt + i,), memory_space=pltpu.SMEM),
                  pl.BlockSpec(memory_space=pl.ANY),
                  pl.BlockSpec((1, tc, TOP_K), lambda b, i: (b, i, 0)),
                  pl.BlockSpec((1, tc, D), lambda b, i: (b, i, 0)),
                  pl.BlockSpec(mod_blk, mod_map)],
        out_specs=pl.BlockSpec((1, tc, D), lambda b, i: (b, i, 0)),
        scratch_shapes=[pltpu.VMEM((TOP_K, tc, D), F32), pltpu.SemaphoreType.DMA(())],
        compiler_params=_cparams(("arbitrary", "arbitrary")),
        name="moe_combine",
    )(dest_flat, yb, gate, x1, gf)


def _moe(h2, e_idx, gate, x1, gf, wgu, bgu, wd, bd, blk, t_tile):
    B, S, D = x1.shape
    T = B * S
    E = wgu.shape[0]
    rank, counts = _ranks(e_idx.reshape(T, TOP_K), t_tile)
    counts = counts[0, :E].astype(I32)
    padded = (counts + blk - 1) // blk * blk
    pad_end = jnp.cumsum(padded)
    pad_start = pad_end - padded
    dest = (pad_start[e_idx.reshape(T, TOP_K)] + rank).reshape(T * TOP_K)
    n_blk = (T * TOP_K + E * (blk - 1) + blk - 1) // blk
    blk_e = jnp.minimum(jnp.searchsorted(pad_end, jnp.arange(n_blk, dtype=I32) * blk, side='right'),
                        E - 1).astype(I32)
    n_used = (pad_end[-1] // blk).astype(I32).reshape(1)
    xb = _scatter(dest, h2.reshape(T, D), jnp.zeros((n_blk * blk, D), F32), t_tile)
    yb = _experts(blk_e, n_used, xb, wgu, bgu, wd, bd, blk)
    return _combine(dest, yb, gate, x1, gf, t_tile)


def _alibi_slopes(n):
    return jnp.asarray([2.0 ** (-8.0 * (i + 1) / n) for i in range(n)], dtype=F32)


def _lambda_init(layer):
    return 0.8 - 0.6 * math.exp(-0.3 * layer)


def _tile(n, want):
    return want if n % want == 0 else n


def kernel(x_prompt, x_sample, cache_diff_k, cache_diff_v, cache_dsa_k, cache_dsa_v, cache_idx_k, page_table,
           c_prompt, c_sample, w_ada, b_ada, norm_attn_g, w_in, q_norm_diff_g, k_norm_diff_g, q_norm_dsa_g,
           k_norm_dsa_g, diff_lambda_qk, diff_subln_g, w_out, norm_ffn_g, w_router, b_router, w_gate_up,
           b_gate_up, w_down, b_down):
    depth = w_ada.shape[0]
    B, S, D = x_prompt.shape
    DB, DQ, _ = x_sample.shape
    assert DQ == 1, "sample path implements single-token decode"
    n_pool, page = cache_diff_k.shape[1:3]
    n_pg = page_table.shape[1]
    E = w_router.shape[-1]
    d_ff = w_down.shape[2]
    slopes_diff = _alibi_slopes(H_DIFF)
    slopes_dsa = _alibi_slopes(H_DSA)
    pt_flat = page_table.reshape(-1).astype(I32)
    pages_per_step = 8 if n_pg % 8 == 0 else 1

    pool_dk = jnp.transpose(cache_diff_k, (0, 1, 3, 4, 5, 2)).reshape(depth * n_pool, SEG, page)
    pool_dv = cache_diff_v.reshape(depth * n_pool, page, H_DIFF, 2 * D_HEAD)
    pool_sk = jnp.transpose(cache_dsa_k, (0, 1, 3, 4, 2)).reshape(depth * n_pool, SEG, page)
    pool_sv = jnp.transpose(cache_dsa_v, (0, 1, 3, 4, 2)).reshape(depth * n_pool, SEG, page)
    pool_ik = jnp.transpose(cache_idx_k, (0, 1, 3, 2)).reshape(depth * n_pool, D_IDX, page)

    head_block = (jnp.arange(SEG)[:, None] // D_HEAD) == (jnp.arange(SEG)[None, :] // D_HEAD)
    bd = jnp.where(head_block, 1.0 / D_HEAD, 0.0).astype(BF16)

    xp = x_prompt
    xs = x_sample.reshape(1, DB, D)
    new_p = [[] for _ in range(5)]
    new_s = [[] for _ in range(5)]
    tm = _tile(S, 512)
    for l in range(depth):
        lam_init = _lambda_init(l)
        mod = _ada(jnp.concatenate([c_prompt, c_sample], axis=0), w_ada[l], b_ada[l])
        mod_p = [m.reshape(B, 1, D) for m in jnp.split(mod[:B], 6, axis=-1)]
        mod_s = [m.reshape(1, DB, D) for m in jnp.split(mod[B:], 6, axis=-1)]

        w_pad = jnp.zeros((D, N_IN_PAD), F32)
        w_pad = w_pad.at[:, :OFF_IK + D_IDX].set(w_in[l][:, :OFF_IK + D_IDX])
        w_pad = w_pad.at[:, OFF_IW:OFF_IW + H_IDX].set(w_in[l][:, OFF_IK + D_IDX:]).astype(BF16)
        tile8 = lambda g: jnp.tile(g, SEG // D_HEAD)
        gains = jnp.stack([tile8(q_norm_diff_g[l]), tile8(k_norm_diff_g[l]),
                           tile8(q_norm_dsa_g[l]), tile8(k_norm_dsa_g[l])])
        g_attn = norm_attn_g[l].reshape(1, D)
        lam_p = diff_lambda_qk[l]
        subg = diff_subln_g[l].reshape(1, 2 * D_HEAD)
        wo = w_out[l].astype(BF16)
        wr = jnp.zeros((D, LANES), F32).at[:, :E].set(w_router[l])
        wr_hi = wr.astype(BF16)
        wr_lo = (wr - wr_hi.astype(F32)).astype(BF16)
        br = jnp.zeros((1, LANES), F32).at[0, :E].set(b_router[l])
        g_ffn = norm_ffn_g[l].reshape(1, D)
        wgu = w_gate_up[l].astype(BF16)
        bgu = b_gate_up[l].reshape(E, 1, 2 * d_ff)
        wd = w_down[l].astype(BF16)
        bdn = b_down[l].reshape(E, 1, D)

        (dq, dk, dkb, dv, dvb, sq, sk, skb, sv, svb, iq, ik, ikb, iw) = _inproj(
            xp, mod_p[1], mod_p[0], g_attn, w_pad, bd, gains, tm)
        o_diff = _diff_prompt(slopes_diff, dq, dkb, dvb, lam_p, subg, lam_init, _tile(S, 512), _tile(S, 512))
        o_dsa = _dsa_prompt(slopes_dsa, sq, iq, iw, skb, svb, ikb, _tile(S, 256), _tile(S, 512))
        x1, h2, e_idx, gate = _outproj(o_diff, o_dsa, xp, mod_p[2], mod_p[4], mod_p[3], g_ffn,
                                       wo, wr_hi, wr_lo, br, E, tm)
        xp = _moe(h2, e_idx, gate, x1, mod_p[5], wgu, bgu, wd, bdn, 512, _tile(S, 256))
        for j, a in enumerate((dk.reshape(B, S, H_DIFF, 2, D_HEAD), dv.reshape(B, S, H_DIFF, 2 * D_HEAD),
                               sk.reshape(B, S, H_DSA, D_HEAD), sv.reshape(B, S, H_DSA, D_HEAD), ik)):
            new_p[j].append(a)

        (dq, dk, dkb, dv, dvb, sq, sk, skb, sv, svb, iq, ik, ikb, iw) = _inproj(
            xs, mod_s[1], mod_s[0], g_attn, w_pad, bd, gains, DB)
        base = l * n_pool
        scores, score_new = _sample_index(
            pt_flat, iq.reshape(DB, H_IDX, D_IDX), iw.reshape(DB, H_IDX, 1), ik.reshape(DB, 1, D_IDX),
            pool_ik, n_pg, pages_per_step, base)
        bias, newok = _sample_select(scores.reshape(DB, n_pg * page), score_new.reshape(DB, 1),
                                     _tile(n_pg * page, 512))
        row_slope_diff = jnp.repeat(slopes_diff, 2).reshape(2 * H_DIFF, 1)
        o_diff = _paged_attn(
            pt_flat, _block_rows(dq[0], D_HEAD), row_slope_diff, dk.reshape(DB, 1, SEG), dv.reshape(DB, 1, SEG),
            jnp.ones((DB, 1, 1), F32), jnp.zeros((DB, 1, n_pg * page), F32), pool_dk, pool_dv,
            (lam_p, subg), functools.partial(_diff_sample_epilogue, lam_init=lam_init),
            n_pg, pages_per_step, base, "diff_sample")
        o_dsa = _paged_attn(
            pt_flat, _block_rows(sq[0], D_HEAD), slopes_dsa.reshape(H_DSA, 1), sk.reshape(DB, 1, SEG),
            sv.reshape(DB, 1, SEG), newok.reshape(DB, 1, 1), bias.reshape(DB, 1, n_pg * page), pool_sk, pool_sv,
            (), _dsa_sample_epilogue, n_pg, pages_per_step, base, "dsa_sample")
        x1, h2, e_idx, gate = _outproj(o_diff.reshape(1, DB, SEG).astype(BF16), o_dsa.reshape(1, DB, SEG).astype(BF16),
                                       xs, mod_s[2], mod_s[4], mod_s[3], g_ffn, wo, wr_hi, wr_lo, br, E, DB)
        xs = _moe(h2, e_idx, gate, x1, mod_s[5], wgu, bgu, wd, bdn, 128, DB)
        for j, a in enumerate((dk.reshape(DB, 1, H_DIFF, 2, D_HEAD), dv.reshape(DB, 1, H_DIFF, 2 * D_HEAD),
                               sk.reshape(DB, 1, H_DSA, D_HEAD), sv.reshape(DB, 1, H_DSA, D_HEAD),
                               ik.reshape(DB, 1, D_IDX))):
            new_s[j].append(a)

    return (xp, xs.reshape(DB, 1, D),
            jnp.stack(new_p[0]), jnp.stack(new_p[1]), jnp.stack(new_p[2]), jnp.stack(new_p[3]), jnp.stack(new_p[4]),
            jnp.stack(new_s[0]), jnp.stack(new_s[1]), jnp.stack(new_s[2]), jnp.stack(new_s[3]), jnp.stack(new_s[4]))
```

```python
import functools
import math

import jax
import jax.numpy as jnp
from jax import lax
from jax.experimental import pallas as pl
from jax.experimental.pallas import tpu as pltpu

F32 = jnp.float32
BF16 = jnp.bfloat16
I32 = jnp.int32

D_HEAD = 64
H_DIFF = 4
H_DSA = 8
H_IDX = 8
D_IDX = 64
TOPK_KEYS = 256
TOP_K = 4
SWIGLU_LIMIT = 7.0
SWIGLU_ALPHA = 1.702
EPS = 1e-6

LANES = 128
SEG = 512
NEG = -1e30
INT_MIN = -2 ** 31
INT_MAX = 2 ** 31 - 1
VMEM_LIMIT = 56 * 1024 * 1024

OFF_IK = 7 * SEG
OFF_IW = OFF_IK + LANES
N_IN_PAD = OFF_IW + LANES


def _cparams(sem, vmem=VMEM_LIMIT):
    return pltpu.CompilerParams(dimension_semantics=sem, vmem_limit_bytes=vmem)


def _dot(a, b):
    return jnp.dot(a, b, preferred_element_type=F32)


def _dot_nt(a, b):
    return lax.dot_general(a, b, (((1,), (1,)), ((), ())), preferred_element_type=F32)


def _split_bf16(a):
    hi = a.astype(BF16)
    lo = (a - hi.astype(F32)).astype(BF16)
    return hi, lo


def _dot3(a, b):
    ah, al = _split_bf16(a)
    bh, bl = _split_bf16(b)
    return _dot(ah, bh) + _dot(al, bh) + _dot(ah, bl)


def _rms(x):
    return x * lax.rsqrt(jnp.mean(x * x, axis=-1, keepdims=True) + EPS)


def _ada_kernel(c_ref, w_ref, b_ref, o_ref):
    c = c_ref[...]
    o_ref[...] = _dot3(c * jax.nn.sigmoid(c), w_ref[...]) + b_ref[...]


def _ada(c, w, b):
    rows, d = c.shape
    n = w.shape[1]
    tn = 1024 if n % 1024 == 0 else n
    return pl.pallas_call(
        _ada_kernel,
        out_shape=jax.ShapeDtypeStruct((rows, n), F32),
        grid=(n // tn,),
        in_specs=[pl.BlockSpec((rows, d), lambda j: (0, 0)),
                  pl.BlockSpec((d, tn), lambda j: (0, j)),
                  pl.BlockSpec((1, tn), lambda j: (0, j))],
        out_specs=pl.BlockSpec((rows, tn), lambda j: (0, j)),
        compiler_params=_cparams(("arbitrary",)),
        name="ada",
    )(c, w, b.reshape(1, n))


def _inproj_kernel(x_ref, sc_ref, sh_ref, g_ref, w_ref, bd_ref, gains_ref,
                   dq_ref, dk_ref, dkb_ref, dv_ref, dvb_ref,
                   sq_ref, sk_ref, skb_ref, sv_ref, svb_ref,
                   iq_ref, ik_ref, ikb_ref, iw_ref):
    x = x_ref[0]
    h = _rms(x) * g_ref[...] * (1.0 + sc_ref[0]) + sh_ref[0]
    hb = h.astype(BF16)

    def seg(i):
        return _dot(hb, w_ref[:, i * SEG:(i + 1) * SEG])

    def head_norm(y, row):
        ms = _dot((y * y).astype(BF16), bd_ref[...])
        return y * lax.rsqrt(ms + EPS) * gains_ref[row:row + 1, :]

    def with_ones(v):
        vb = v.astype(BF16)
        ones = jnp.ones((vb.shape[0], LANES), BF16)
        parts = []
        for i in range(SEG // LANES):
            parts += [vb[:, i * LANES:(i + 1) * LANES], ones]
        return jnp.concatenate(parts, axis=-1)

    scale = D_HEAD ** -0.5
    dq_ref[0] = (head_norm(seg(0), 0) * scale).astype(BF16)
    dk = head_norm(seg(1), 1)
    dk_ref[0] = dk
    dkb_ref[0] = dk.astype(BF16)
    dv = seg(2)
    dv_ref[0] = dv
    dvb_ref[0] = with_ones(dv)
    sq_ref[0] = (head_norm(seg(3), 2) * scale).astype(BF16)
    sk = head_norm(seg(4), 3)
    sk_ref[0] = sk
    skb_ref[0] = sk.astype(BF16)
    sv = seg(5)
    sv_ref[0] = sv
    svb_ref[0] = with_ones(sv)
    iq_ref[0] = seg(6).astype(BF16)
    ik = _dot(hb, w_ref[:, OFF_IK:OFF_IK + LANES])[:, :D_IDX]
    ik_ref[0] = ik
    ikb_ref[0] = ik.astype(BF16)
    iw_ref[0] = _dot(hb, w_ref[:, OFF_IW:OFF_IW + LANES])[:, :H_IDX] * (H_IDX ** -0.5)


def _inproj(x, sc, sh, g, w_pad, bd, gains, tm):
    B, S, D = x.shape
    mod_rows = sc.shape[1]
    mod_blk = (1, 1, D) if mod_rows == 1 else (1, tm, D)
    mod_map = (lambda b, i: (b, 0, 0)) if mod_rows == 1 else (lambda b, i: (b, i, 0))
    row = lambda w: pl.BlockSpec((1, tm, w), lambda b, i: (b, i, 0))
    const = lambda shp: pl.BlockSpec(shp, lambda b, i: (0,) * len(shp))
    sds = lambda w, dt: jax.ShapeDtypeStruct((B, S, w), dt)
    outs = [(SEG, BF16), (SEG, F32), (SEG, BF16), (SEG, F32), (2 * SEG, BF16),
            (SEG, BF16), (SEG, F32), (SEG, BF16), (SEG, F32), (2 * SEG, BF16),
            (SEG, BF16), (D_IDX, F32), (D_IDX, BF16), (H_IDX, F32)]
    return pl.pallas_call(
        _inproj_kernel,
        out_shape=[sds(w, dt) for w, dt in outs],
        grid=(B, S // tm),
        in_specs=[row(D), pl.BlockSpec(mod_blk, mod_map), pl.BlockSpec(mod_blk, mod_map),
                  const((1, D)), const((D, N_IN_PAD)), const((SEG, SEG)), const((4, SEG))],
        out_specs=[row(w) for w, _ in outs],
        compiler_params=_cparams(("parallel", "parallel")),
        name="inproj",
    )(x, sc, sh, g, w_pad, bd, gains)


def _diff_prompt_kernel(slopes_ref, q_ref, k_ref, v_ref, lam_ref, subg_ref, o_ref,
                        m_sc, acc_sc, *, tq, tk, lam_init):
    h = pl.program_id(1)
    q0 = pl.program_id(2) * tq
    slope = slopes_ref[h]
    q = q_ref[0]
    lane = lax.broadcasted_iota(I32, (1, LANES), 1)
    qs = (jnp.where(lane < D_HEAD, q, jnp.zeros_like(q)),
          jnp.where(lane >= D_HEAD, q, jnp.zeros_like(q)))
    rowpos = q0 + lax.broadcasted_iota(I32, (tq, 1), 0)
    col = lax.broadcasted_iota(I32, (1, tk), 1)

    m_sc[...] = jnp.full(m_sc.shape, NEG, F32)
    acc_sc[...] = jnp.zeros(acc_sc.shape, F32)

    def chunk(c, masked):
        start = pl.multiple_of(c * tk, tk)
        k = k_ref[0, pl.ds(start, tk), :]
        v = v_ref[0, pl.ds(start, tk), :]
        colpos = start + col
        bias = slope * (colpos - q0).astype(F32)
        for j in range(2):
            s = _dot_nt(qs[j], k) + bias
            if masked:
                s = jnp.where(colpos <= rowpos, s, NEG)
            m_old = m_sc[j]
            m_new = jnp.maximum(m_old, jnp.max(s, axis=-1, keepdims=True))
            alpha = jnp.exp(m_old - m_new)
            p = jnp.exp(s - m_new)
            acc_sc[j] = alpha * acc_sc[j] + _dot(p.astype(BF16), v)
            m_sc[j] = m_new

    n_full = q0 // tk
    n_all = (q0 + tq + tk - 1) // tk

    def full_body(c, carry):
        chunk(c, False)
        return carry

    def diag_body(c, carry):
        chunk(c, True)
        return carry

    lax.fori_loop(0, n_full, full_body, 0)
    lax.fori_loop(n_full, n_all, diag_body, 0)

    lp = lam_ref[...]
    lam = (jnp.exp(jnp.sum(lp[0:1] * lp[1:2], axis=-1, keepdims=True))
           - jnp.exp(jnp.sum(lp[2:3] * lp[3:4], axis=-1, keepdims=True)) + lam_init)
    o = (acc_sc[0, :, :LANES] / acc_sc[0, :, LANES:]
         - lam * (acc_sc[1, :, :LANES] / acc_sc[1, :, LANES:]))
    o_ref[0] = (_rms(o) * subg_ref[...] * (1.0 - lam_init)).astype(BF16)


def _diff_prompt(slopes, dq, dkb, dvb, lam_p, subg, lam_init, tq, tk):
    B, S, _ = dq.shape
    kern = functools.partial(_diff_prompt_kernel, tq=tq, tk=tk, lam_init=lam_init)
    return pl.pallas_call(
        kern,
        out_shape=jax.ShapeDtypeStruct((B, S, SEG), BF16),
        grid=(B, H_DIFF, S // tq),
        in_specs=[pl.BlockSpec(memory_space=pltpu.SMEM),
                  pl.BlockSpec((1, tq, LANES), lambda b, h, i: (b, i, h)),
                  pl.BlockSpec((1, S, LANES), lambda b, h, i: (b, 0, h)),
                  pl.BlockSpec((1, S, 2 * LANES), lambda b, h, i: (b, 0, h)),
                  pl.BlockSpec((4, D_HEAD), lambda b, h, i: (0, 0)),
                  pl.BlockSpec((1, LANES), lambda b, h, i: (0, 0))],
        out_specs=pl.BlockSpec((1, tq, LANES), lambda b, h, i: (b, i, h)),
        scratch_shapes=[pltpu.VMEM((2, tq, 1), F32), pltpu.VMEM((2, tq, 2 * LANES), F32)],
        compiler_params=_cparams(("parallel", "parallel", "parallel")),
        name="diff_prompt",
    )(slopes, dq, dkb, dvb, lam_p, subg)


def _sortable(x):
    x = jnp.where(x == 0.0, jnp.zeros_like(x), x)
    bits = pltpu.bitcast(x, I32)
    return bits ^ ((bits >> 31) & INT_MAX)


def _count(key_sc, n_chunks, tk, rows, pred, extra=None):
    def body(c, acc):
        ind = jnp.where(pred(key_sc[c], c), 1.0, 0.0)
        part = ind[:, 0:LANES]
        for j in range(1, tk // LANES):
            part = part + ind[:, j * LANES:(j + 1) * LANES]
        return acc + part

    acc = lax.fori_loop(0, n_chunks, body, jnp.zeros((rows, LANES), F32))
    cnt = jnp.sum(acc, axis=-1, keepdims=True)
    if extra is not None:
        cnt = cnt + jnp.where(extra, 1.0, 0.0)
    return cnt


def _topk_select(key_sc, n_chunks, tk, rows, k_sel, idx_bits, cut_sc, key_new=None, idx_new=None):
    col = lax.broadcasted_iota(I32, (1, tk), 1)

    def count(pred, pred_new=None):
        extra = None if key_new is None else pred_new
        return _count(key_sc, n_chunks, tk, rows, pred, extra)

    def count_ge(cand):
        return count(lambda kc, c: kc >= cand, None if key_new is None else key_new >= cand)

    kf = float(k_sel)
    thr = jnp.where(count_ge(jnp.zeros((rows, 1), I32)) >= kf, 0, INT_MIN).astype(I32)

    def bit_body(i, thr):
        cand = thr | lax.shift_left(jnp.int32(1), 30 - i)
        return jnp.where(count_ge(cand) >= kf, cand, thr)

    thr = lax.fori_loop(0, 31, bit_body, thr)

    cnt_gt = count(lambda kc, c: kc > thr, None if key_new is None else key_new > thr)
    cnt_ge = count_ge(thr)
    need = kf - cnt_gt
    tie = (cnt_ge - cnt_gt) > need
    cut_sc[...] = jnp.full((rows, 1), INT_MAX, I32)

    @pl.when(jnp.max(jnp.where(tie, 1.0, 0.0)) > 0.0)
    def _():
        def idx_body(i, cut):
            cand = cut | lax.shift_left(jnp.int32(1), idx_bits - 1 - i)
            cnt = count(lambda kc, c: (kc == thr) & (c * tk + col < cand),
                        None if key_new is None else (key_new == thr) & (idx_new < cand))
            return jnp.where(cnt < need, cand, cut)

        cut = lax.fori_loop(0, idx_bits, idx_body, jnp.zeros((rows, 1), I32))
        cut_sc[...] = jnp.where(tie, cut, INT_MAX)

    return thr, cut_sc[...]


def _dsa_prompt_kernel(slopes_ref, q_ref, iq_ref, iw_ref, k_ref, v_ref, ik_ref, o_ref,
                       key_sc, bias_sc, cut_sc, m_sc, acc_sc, *, tq, tk, k_sel, idx_bits):
    q0 = pl.program_id(1) * tq
    n_chunks = (q0 + tq + tk - 1) // tk
    rowpos = q0 + lax.broadcasted_iota(I32, (tq, 1), 0)
    col = lax.broadcasted_iota(I32, (1, tk), 1)

    def idx_body(c, carry):
        start = pl.multiple_of(c * tk, tk)
        ikc = ik_ref[0, pl.ds(start, tk), :]
        acc = jnp.zeros((tq, tk), F32)
        for h in range(H_IDX):
            r = jnp.maximum(_dot_nt(iq_ref[0, :, h * D_IDX:(h + 1) * D_IDX], ikc), 0.0)
            acc = acc + r * iw_ref[0, :, h:h + 1]
        acc = jnp.where(start + col <= rowpos, acc, -jnp.inf)
        key_sc[c] = _sortable(acc)
        return carry

    lax.fori_loop(0, n_chunks, idx_body, 0)

    thr, cut = _topk_select(key_sc, n_chunks, tk, tq, k_sel, idx_bits, cut_sc)

    def mask_body(c, carry):
        kc = key_sc[c]
        idx = c * tk + col
        sel = (kc > thr) | ((kc == thr) & (idx <= cut))
        bias_sc[c] = jnp.where(sel & (idx <= rowpos), 0.0, NEG)
        return carry

    lax.fori_loop(0, n_chunks, mask_body, 0)

    m_sc[...] = jnp.full(m_sc.shape, NEG, F32)
    acc_sc[...] = jnp.zeros(acc_sc.shape, F32)
    lane = lax.broadcasted_iota(I32, (1, LANES), 1)
    is_lo = lane < D_HEAD

    def att_body(c, carry):
        start = pl.multiple_of(c * tk, tk)
        mb = bias_sc[c]
        rel = (start - q0 + col).astype(F32)
        for p in range(H_DSA // 2):
            qp = q_ref[0, :, p * LANES:(p + 1) * LANES]
            kp = k_ref[0, pl.ds(start, tk), p * LANES:(p + 1) * LANES]
            vp = v_ref[0, pl.ds(start, tk), 2 * p * LANES:2 * (p + 1) * LANES]
            for half in range(2):
                h = 2 * p + half
                qh = jnp.where(is_lo if half == 0 else ~is_lo, qp, jnp.zeros_like(qp))
                s = _dot_nt(qh, kp) + (mb + slopes_ref[h] * rel)
                m_old = m_sc[h]
                m_new = jnp.maximum(m_old, jnp.max(s, axis=-1, keepdims=True))
                alpha = jnp.exp(m_old - m_new)
                pr = jnp.exp(s - m_new)
                m_sc[h] = m_new
                acc_sc[h] = alpha * acc_sc[h] + _dot(pr.astype(BF16), vp)
        return carry

    lax.fori_loop(0, n_chunks, att_body, 0)

    for p in range(H_DSA // 2):
        lo = acc_sc[2 * p, :, :LANES] / acc_sc[2 * p, :, LANES:]
        hi = acc_sc[2 * p + 1, :, :LANES] / acc_sc[2 * p + 1, :, LANES:]
        o_ref[0, :, p * LANES:(p + 1) * LANES] = jnp.where(is_lo, lo, hi).astype(BF16)


def _dsa_prompt(slopes, sq, iq, iw, skb, svb, ikb, tq, tk):
    B, S, _ = sq.shape
    k_sel = min(TOPK_KEYS, S // 4)
    kern = functools.partial(_dsa_prompt_kernel, tq=tq, tk=tk, k_sel=k_sel,
                             idx_bits=max(1, (S - 1).bit_length()))
    nck = S // tk
    return pl.pallas_call(
        kern,
        out_shape=jax.ShapeDtypeStruct((B, S, SEG), BF16),
        grid=(B, S // tq),
        in_specs=[pl.BlockSpec(memory_space=pltpu.SMEM),
                  pl.BlockSpec((1, tq, SEG), lambda b, i: (b, i, 0)),
                  pl.BlockSpec((1, tq, SEG), lambda b, i: (b, i, 0)),
                  pl.BlockSpec((1, tq, H_IDX), lambda b, i: (b, i, 0)),
                  pl.BlockSpec((1, S, SEG), lambda b, i: (b, 0, 0)),
                  pl.BlockSpec((1, S, 2 * SEG), lambda b, i: (b, 0, 0)),
                  pl.BlockSpec((1, S, D_IDX), lambda b, i: (b, 0, 0))],
        out_specs=pl.BlockSpec((1, tq, SEG), lambda b, i: (b, i, 0)),
        scratch_shapes=[pltpu.VMEM((nck, tq, tk), I32), pltpu.VMEM((nck, tq, tk), F32),
                        pltpu.VMEM((tq, 1), I32),
                        pltpu.VMEM((H_DSA, tq, 1), F32),
                        pltpu.VMEM((H_DSA, tq, 2 * LANES), F32)],
        compiler_params=_cparams(("parallel", "parallel")),
        name="dsa_prompt",
    )(slopes, sq, iq, iw, skb, svb, ikb)


def _sample_index_kernel(pt_ref, iq_ref, iw_ref, iknew_ref, *refs, n_pages):
    page_refs = refs[:n_pages]
    o_ref, onew_ref = refs[n_pages:]
    iq = iq_ref[0]
    iw = iw_ref[0]
    r = jnp.concatenate([_dot(iq, page_refs[i][0].astype(BF16)) for i in range(n_pages)], axis=-1)
    o_ref[0] = jnp.sum(jnp.maximum(r, 0.0) * iw, axis=0, keepdims=True)

    @pl.when(pl.program_id(1) == 0)
    def _():
        kn = iknew_ref[0].astype(BF16).astype(F32)
        rn = jnp.maximum(jnp.sum(iq.astype(F32) * kn, axis=-1, keepdims=True), 0.0)
        onew_ref[0] = jnp.sum(rn * iw, axis=0, keepdims=True)


def _sample_index(pt_flat, iq, iw, ik_new, pool, n_pg, n_pages, page_base):
    DB = iq.shape[0]
    page = pool.shape[2]
    assert page == LANES
    kern = functools.partial(_sample_index_kernel, n_pages=n_pages)

    def page_spec(i):
        return pl.BlockSpec((1, D_IDX, page),
                            lambda b, g, pt: (page_base + pt[b * n_pg + g * n_pages + i], 0, 0))

    grid_spec = pltpu.PrefetchScalarGridSpec(
        num_scalar_prefetch=1,
        grid=(DB, n_pg // n_pages),
        in_specs=[pl.BlockSpec((1, H_IDX, D_IDX), lambda b, g, pt: (b, 0, 0)),
                  pl.BlockSpec((1, H_IDX, 1), lambda b, g, pt: (b, 0, 0)),
                  pl.BlockSpec((1, 1, D_IDX), lambda b, g, pt: (b, 0, 0))]
                 + [page_spec(i) for i in range(n_pages)],
        out_specs=[pl.BlockSpec((1, 1, n_pages * page), lambda b, g, pt: (b, 0, g)),
                   pl.BlockSpec((1, 1, 1), lambda b, g, pt: (b, 0, 0))],
    )
    return pl.pallas_call(
        kern,
        out_shape=[jax.ShapeDtypeStruct((DB, 1, n_pg * page), F32),
                   jax.ShapeDtypeStruct((DB, 1, 1), F32)],
        grid_spec=grid_spec,
        compiler_params=_cparams(("parallel", "arbitrary")),
        name="sample_index",
    )(pt_flat, iq, iw, ik_new, *([pool] * n_pages))


def _sample_select_kernel(s_ref, snew_ref, bias_ref, newok_ref, key_sc, cut_sc,
                          *, n_chunks, tk, rows, k_sel, idx_bits):
    def to_keys(c, carry):
        key_sc[c] = _sortable(s_ref[c])
        return carry

    lax.fori_loop(0, n_chunks, to_keys, 0)
    key_new = _sortable(snew_ref[...])
    idx_new = jnp.full((rows, 1), n_chunks * tk, I32)
    thr, cut = _topk_select(key_sc, n_chunks, tk, rows, k_sel, idx_bits, cut_sc,
                            key_new=key_new, idx_new=idx_new)
    col = lax.broadcasted_iota(I32, (1, tk), 1)

    def mask_body(c, carry):
        kc = key_sc[c]
        sel = (kc > thr) | ((kc == thr) & (c * tk + col <= cut))
        bias_ref[c] = jnp.where(sel, 0.0, NEG)
        return carry

    lax.fori_loop(0, n_chunks, mask_body, 0)
    sel_new = (key_new > thr) | ((key_new == thr) & (idx_new <= cut))
    newok_ref[...] = jnp.where(sel_new, 1.0, 0.0)


def _sample_select(scores, score_new, tk):
    DB, P = scores.shape
    n_chunks = P // tk
    k_sel = min(TOPK_KEYS, (P + 1) // 4)
    s3 = scores.reshape(DB, n_chunks, tk).transpose(1, 0, 2)
    kern = functools.partial(_sample_select_kernel, n_chunks=n_chunks, tk=tk, rows=DB, k_sel=k_sel,
                             idx_bits=max(1, P.bit_length()))
    bias3, newok = pl.pallas_call(
        kern,
        out_shape=[jax.ShapeDtypeStruct((n_chunks, DB, tk), F32), jax.ShapeDtypeStruct((DB, 1), F32)],
        scratch_shapes=[pltpu.VMEM((n_chunks, DB, tk), I32), pltpu.VMEM((DB, 1), I32)],
        compiler_params=pltpu.CompilerParams(vmem_limit_bytes=VMEM_LIMIT),
        name="sample_select",
    )(s3, score_new)
    return bias3.transpose(1, 0, 2).reshape(DB, P), newok


def _paged_attn_kernel(pt_ref, q_ref, slope_ref, knew_ref, vnew_ref, newok_ref, bias_ref, *refs,
                       n_pages, n_rows, past_len, v_transposed, epilogue):
    k_refs = refs[:n_pages]
    v_refs = refs[n_pages:2 * n_pages]
    extra_refs = refs[2 * n_pages:-4]
    o_ref, m_sc, l_sc, acc_sc = refs[-4:]
    g = pl.program_id(1)
    q = q_ref[0]
    page = LANES

    @pl.when(g == 0)
    def _():
        kn = knew_ref[0].astype(BF16).astype(F32)
        ok = newok_ref[0] > 0.5
        s_new = jnp.sum(q.astype(F32) * kn, axis=-1, keepdims=True)
        m_sc[...] = jnp.where(ok, s_new, NEG)
        l_sc[...] = jnp.where(ok, jnp.ones((n_rows, 1), F32), 0.0)
        vn = vnew_ref[0].astype(BF16).astype(F32)
        acc_sc[...] = jnp.where(ok, jnp.broadcast_to(vn, (n_rows, SEG)), 0.0)

    width = n_pages * page
    pos = g * width + lax.broadcasted_iota(I32, (1, width), 1)
    rel = (pos - past_len).astype(F32)
    s = jnp.concatenate([_dot(q, k_refs[i][0].astype(BF16)) for i in range(n_pages)], axis=-1)
    s = s + slope_ref[...] * rel + bias_ref[0]
    m_old = m_sc[...]
    m_new = jnp.maximum(m_old, jnp.max(s, axis=-1, keepdims=True))
    alpha = jnp.exp(m_old - m_new)
    p = jnp.exp(s - m_new)
    l_sc[...] = alpha * l_sc[...] + jnp.sum(p, axis=-1, keepdims=True)
    p = p.astype(BF16)
    pv = jnp.zeros((n_rows, SEG), F32)
    for i in range(n_pages):
        pi = p[:, i * page:(i + 1) * page]
        if v_transposed:
            pv = pv + _dot_nt(pi, v_refs[i][0].astype(BF16))
        else:
            n_h = v_refs[i].shape[1] // page
            pv = pv + jnp.concatenate(
                [_dot(pi, v_refs[i][0, pl.ds(h, page, stride=n_h), :].astype(BF16)) for h in range(n_h)], axis=-1)
    acc_sc[...] = alpha * acc_sc[...] + pv
    m_sc[...] = m_new

    @pl.when(g == pl.num_programs(1) - 1)
    def _():
        o_ref[0] = epilogue(acc_sc[...] / l_sc[...], *extra_refs)


def _diff_sample_epilogue(on, lam_ref, subg_ref, *, lam_init):
    row = lax.broadcasted_iota(I32, on.shape, 0)
    colh = lax.broadcasted_iota(I32, on.shape, 1) // (2 * D_HEAD)
    own = colh == row // 2
    o1 = jnp.sum(jnp.where(own & (row % 2 == 0), on, 0.0), axis=0, keepdims=True)
    o2 = jnp.sum(jnp.where(own & (row % 2 == 1), on, 0.0), axis=0, keepdims=True)
    lp = lam_ref[...]
    lam = (jnp.exp(jnp.sum(lp[0:1] * lp[1:2], axis=-1, keepdims=True))
           - jnp.exp(jnp.sum(lp[2:3] * lp[3:4], axis=-1, keepdims=True)) + lam_init)
    o = o1 - lam * o2
    w = 2 * D_HEAD
    segs = [_rms(o[:, h * w:(h + 1) * w]) * subg_ref[...] * (1.0 - lam_init) for h in range(H_DIFF)]
    return jnp.concatenate(segs, axis=-1)


def _dsa_sample_epilogue(on):
    row = lax.broadcasted_iota(I32, on.shape, 0)
    colh = lax.broadcasted_iota(I32, on.shape, 1) // D_HEAD
    return jnp.sum(jnp.where(colh == row, on, 0.0), axis=0, keepdims=True)


def _paged_attn(pt_flat, qmat, row_slope, k_new, v_new, newok, bias, k_pool, v_pool, v_transposed, extras,
                epilogue, n_pg, n_pages, page_base, name):
    DB, R, _ = qmat.shape
    page = k_pool.shape[2]
    assert page == LANES
    kern = functools.partial(_paged_attn_kernel, n_pages=n_pages, n_rows=R, past_len=n_pg * page,
                             v_transposed=v_transposed, epilogue=epilogue)

    def page_spec(i, pool):
        blk = (1,) + pool.shape[1:]
        zeros = (0,) * (pool.ndim - 1)
        return pl.BlockSpec(blk, lambda b, g, pt: (page_base + pt[b * n_pg + g * n_pages + i],) + zeros)

    per_b = lambda shp: pl.BlockSpec(shp, lambda b, g, pt: (b, 0, 0))
    grid_spec = pltpu.PrefetchScalarGridSpec(
        num_scalar_prefetch=1,
        grid=(DB, n_pg // n_pages),
        in_specs=[per_b((1, R, SEG)), pl.BlockSpec((R, 1), lambda b, g, pt: (0, 0)),
                  per_b((1, 1, SEG)), per_b((1, 1, SEG)), per_b((1, 1, 1)),
                  pl.BlockSpec((1, 1, n_pages * page), lambda b, g, pt: (b, 0, g))]
                 + [page_spec(i, k_pool) for i in range(n_pages)]
                 + [page_spec(i, v_pool) for i in range(n_pages)]
                 + [pl.BlockSpec(e.shape, lambda b, g, pt: (0, 0)) for e in extras],
        out_specs=per_b((1, 1, SEG)),
        scratch_shapes=[pltpu.VMEM((R, 1), F32), pltpu.VMEM((R, 1), F32), pltpu.VMEM((R, SEG), F32)],
    )
    return pl.pallas_call(
        kern,
        out_shape=jax.ShapeDtypeStruct((DB, 1, SEG), F32),
        grid_spec=grid_spec,
        compiler_params=_cparams(("parallel", "arbitrary")),
        name=name,
    )(pt_flat, qmat, row_slope, k_new, v_new, newok, bias,
      *([k_pool] * n_pages), *([v_pool] * n_pages), *extras)


def _block_rows(q, width):
    n = SEG // width
    own = (jnp.arange(SEG)[None, :] // width) == jnp.arange(n)[:, None]
    return jnp.where(own[None], q[:, None, :], jnp.zeros((), q.dtype))


def _outproj_kernel(od_ref, os_ref, x_ref, ga_ref, scf_ref, shf_ref, g_ref, wo_ref, wr_ref, br_ref,
                    x1_ref, h2_ref, e_ref, gate_ref, *, n_experts):
    o = _dot(od_ref[0], wo_ref[0:SEG, :]) + _dot(os_ref[0], wo_ref[SEG:2 * SEG, :])
    x1 = x_ref[0] + ga_ref[0] * o
    x1_ref[0] = x1
    h2 = _rms(x1) * g_ref[...] * (1.0 + scf_ref[0]) + shf_ref[0]
    h2_ref[0] = h2
    logits = _dot(h2.astype(BF16), wr_ref[...]) + br_ref[...]
    lane = lax.broadcasted_iota(I32, logits.shape, 1).astype(F32)
    lg = jnp.where(lane < n_experts, logits, -jnp.inf)
    vals, idxs = [], []
    for _ in range(TOP_K):
        m = jnp.max(lg, axis=-1, keepdims=True)
        idx = jnp.min(jnp.where(lg == m, lane, float(LANES)), axis=-1, keepdims=True)
        lg = jnp.where(lane == idx, -jnp.inf, lg)
        vals.append(m)
        idxs.append(idx)
    ex = [jnp.exp(v - vals[0]) for v in vals]
    tot = ex[0] + ex[1] + ex[2] + ex[3]
    e_ref[0] = jnp.concatenate(idxs, axis=-1).astype(I32)
    gate_ref[0] = jnp.concatenate([e / tot for e in ex], axis=-1)


def _outproj(od, osx, x, ga, scf, shf, g, wo, wr, br, n_experts, tm):
    B, S, D = x.shape
    mod_rows = ga.shape[1]
    mod_blk = (1, 1, D) if mod_rows == 1 else (1, tm, D)
    mod_map = (lambda b, i: (b, 0, 0)) if mod_rows == 1 else (lambda b, i: (b, i, 0))
    row = lambda w: pl.BlockSpec((1, tm, w), lambda b, i: (b, i, 0))
    const = lambda shp: pl.BlockSpec(shp, lambda b, i: (0,) * len(shp))
    kern = functools.partial(_outproj_kernel, n_experts=n_experts)
    return pl.pallas_call(
        kern,
        out_shape=[jax.ShapeDtypeStruct((B, S, D), F32), jax.ShapeDtypeStruct((B, S, D), F32),
                   jax.ShapeDtypeStruct((B, S, TOP_K), I32), jax.ShapeDtypeStruct((B, S, TOP_K), F32)],
        grid=(B, S // tm),
        in_specs=[row(SEG), row(SEG), row(D)] + [pl.BlockSpec(mod_blk, mod_map)] * 3
                 + [const((1, D)), const((2 * SEG, D)), const((D, LANES)), const((1, LANES))],
        out_specs=[row(D), row(D), row(TOP_K), row(TOP_K)],
        compiler_params=_cparams(("parallel", "parallel")),
        name="outproj_router",
    )(od, osx, x, ga, scf, shf, g, wo, wr, br)


def _rank_kernel(e_ref, rank_ref, cnt_ref, carry_sc, *, tr):
    @pl.when(pl.program_id(0) == 0)
    def _():
        carry_sc[...] = jnp.zeros(carry_sc.shape, F32)

    e = e_ref[...]
    lane = lax.broadcasted_iota(I32, (tr, LANES), 1)
    ohs = [lane == e[:, k:k + 1] for k in range(TOP_K)]
    oh = jnp.zeros((tr, LANES), F32)
    for m in ohs:
        oh = oh + jnp.where(m, 1.0, 0.0)
    r = lax.broadcasted_iota(I32, (tr, tr), 0)
    c = lax.broadcasted_iota(I32, (tr, tr), 1)
    lower = jnp.where(r > c, 1.0, 0.0).astype(BF16)
    rank_full = carry_sc[...] + _dot(lower, oh.astype(BF16))
    cols = [jnp.sum(jnp.where(m, rank_full, 0.0), axis=-1, keepdims=True) for m in ohs]
    rank_ref[...] = jnp.concatenate(cols, axis=-1).astype(I32)
    carry_sc[...] = carry_sc[...] + jnp.sum(oh, axis=0, keepdims=True)
    cnt_ref[...] = carry_sc[...]


def _ranks(e_idx, tr):
    T = e_idx.shape[0]
    kern = functools.partial(_rank_kernel, tr=tr)
    return pl.pallas_call(
        kern,
        out_shape=[jax.ShapeDtypeStruct((T, TOP_K), I32), jax.ShapeDtypeStruct((1, LANES), F32)],
        grid=(T // tr,),
        in_specs=[pl.BlockSpec((tr, TOP_K), lambda i: (i, 0))],
        out_specs=[pl.BlockSpec((tr, TOP_K), lambda i: (i, 0)), pl.BlockSpec((1, LANES), lambda i: (0, 0))],
        scratch_shapes=[pltpu.VMEM((1, LANES), F32)],
        compiler_params=_cparams(("arbitrary",)),
        name="moe_rank",
    )(e_idx)


def _row_copy(src_ref, src_row, dst_ref, dst_row, sem):
    return pltpu.make_async_copy(src_ref.at[pl.ds(src_row, 1)], dst_ref.at[pl.ds(dst_row, 1)], sem)


def _scatter_kernel(dest_ref, h_ref, xb_in_ref, xb_ref, sem, *, ts):
    del xb_in_ref

    def start(r, carry):
        for k in range(TOP_K):
            _row_copy(h_ref, r, xb_ref, dest_ref[r * TOP_K + k], sem).start(priority=k % 2)
        return carry

    def wait(r, carry):
        for k in range(TOP_K):
            _row_copy(h_ref, r, xb_ref, dest_ref[r * TOP_K + k], sem).wait()
        return carry

    lax.fori_loop(0, ts, start, 0)
    lax.fori_loop(0, ts, wait, 0)


def _scatter(dest_flat, h2, xb, ts):
    T, D = h2.shape
    kern = functools.partial(_scatter_kernel, ts=ts)
    return pl.pallas_call(
        kern,
        out_shape=jax.ShapeDtypeStruct(xb.shape, xb.dtype),
        grid=(T // ts,),
        in_specs=[pl.BlockSpec((ts * TOP_K,), lambda i: (i,), memory_space=pltpu.SMEM),
                  pl.BlockSpec((ts, D), lambda i: (i, 0)),
                  pl.BlockSpec(memory_space=pl.ANY)],
        out_specs=pl.BlockSpec(memory_space=pl.ANY),
        scratch_shapes=[pltpu.SemaphoreType.DMA(())],
        input_output_aliases={2: 0},
        compiler_params=pltpu.CompilerParams(dimension_semantics=("arbitrary",), vmem_limit_bytes=VMEM_LIMIT,
                                             has_side_effects=True),
        name="moe_scatter",
    )(dest_flat, h2, xb)


def _expert_kernel(blk_e_ref, n_used_ref, x_ref, wgu_ref, bgu_ref, wd_ref, bd_ref, y_ref, *, d_ff, chunk):
    del blk_e_ref

    @pl.when(pl.program_id(0) < n_used_ref[0])
    def _():
        x = x_ref[...].astype(BF16)
        acc = jnp.broadcast_to(bd_ref[0], y_ref.shape)
        for c in range(d_ff // chunk):
            lo = c * chunk
            g = _dot(x, wgu_ref[0, :, lo:lo + chunk]) + bgu_ref[0, :, lo:lo + chunk]
            u = _dot(x, wgu_ref[0, :, d_ff + lo:d_ff + lo + chunk]) + bgu_ref[0, :, d_ff + lo:d_ff + lo + chunk]
            g = jnp.minimum(g, SWIGLU_LIMIT)
            u = jnp.clip(u, -SWIGLU_LIMIT, SWIGLU_LIMIT)
            a = (u + 1.0) * (g * jax.nn.sigmoid(SWIGLU_ALPHA * g))
            acc = acc + _dot(a.astype(BF16), wd_ref[0, lo:lo + chunk, :])
        y_ref[...] = acc

    @pl.when(pl.program_id(0) >= n_used_ref[0])
    def _():
        y_ref[...] = jnp.zeros(y_ref.shape, F32)


def _experts(blk_e, n_used, xb, wgu, bgu, wd, bd, blk):
    n_rows, D = xb.shape
    E, _, two_ff = wgu.shape
    d_ff = two_ff // 2
    chunk = 512 if d_ff % 512 == 0 else d_ff
    kern = functools.partial(_expert_kernel, d_ff=d_ff, chunk=chunk)
    grid_spec = pltpu.PrefetchScalarGridSpec(
        num_scalar_prefetch=2,
        grid=(n_rows // blk,),
        in_specs=[pl.BlockSpec((blk, D), lambda j, be, nu: (j, 0)),
                  pl.BlockSpec((1, D, two_ff), lambda j, be, nu: (be[j], 0, 0)),
                  pl.BlockSpec((1, 1, two_ff), lambda j, be, nu: (be[j], 0, 0)),
                  pl.BlockSpec((1, d_ff, D), lambda j, be, nu: (be[j], 0, 0)),
                  pl.BlockSpec((1, 1, D), lambda j, be, nu: (be[j], 0, 0))],
        out_specs=pl.BlockSpec((blk, D), lambda j, be, nu: (j, 0)),
    )
    return pl.pallas_call(
        kern,
        out_shape=jax.ShapeDtypeStruct((n_rows, D), F32),
        grid_spec=grid_spec,
        compiler_params=_cparams(("arbitrary",)),
        name="moe_experts",
    )(blk_e, n_used, xb, wgu, bgu, wd, bd)


def _combine_kernel(dest_ref, yb_ref, gate_ref, x1_ref, gf_ref, o_ref, buf, sem, *, tc):
    def start(r, carry):
        for k in range(TOP_K):
            _row_copy(yb_ref, dest_ref[r * TOP_K + k], buf.at[k], r, sem).start(priority=k % 2)
        return carry

    def wait(r, carry):
        for k in range(TOP_K):
            _row_copy(yb_ref, dest_ref[r * TOP_K + k], buf.at[k], r, sem).wait()
        return carry

    lax.fori_loop(0, tc, start, 0)
    lax.fori_loop(0, tc, wait, 0)
    gate = gate_ref[0]
    y = gate[:, 0:1] * buf[0]
    for k in range(1, TOP_K):
        y = y + gate[:, k:k + 1] * buf[k]
    o_ref[0] = x1_ref[0] + gf_ref[0] * y


def _combine(dest_flat, yb, gate, x1, gf, tc):
    B, S, D = x1.shape
    nt = S // tc
    mod_rows = gf.shape[1]
    mod_blk = (1, 1, D) if mod_rows == 1 else (1, tc, D)
    mod_map = (lambda b, i: (b, 0, 0)) if mod_rows == 1 else (lambda b, i: (b, i, 0))
    kern = functools.partial(_combine_kernel, tc=tc)
    return pl.pallas_call(
        kern,
        out_shape=jax.ShapeDtypeStruct((B, S, D), F32),
        grid=(B, nt),
        in_specs=[pl.BlockSpec((tc * TOP_K,), lambda b, i: (b * nt + i,), memory_space=pltpu.SMEM),
                  pl.BlockSpec(memory_space=pl.ANY),
                  pl.BlockSpec((1, tc, TOP_K), lambda b, i: (b, i, 0)),
                  pl.BlockSpec((1, tc, D), lambda b, i: (b, i, 0)),
                  pl.BlockSpec(mod_blk, mod_map)],
        out_specs=pl.BlockSpec((1, tc, D), lambda b, i: (b, i, 0)),
        scratch_shapes=[pltpu.VMEM((TOP_K, tc, D), F32), pltpu.SemaphoreType.DMA(())],
        compiler_params=_cparams(("arbitrary", "arbitrary")),
        name="moe_combine",
    )(dest_flat, yb, gate, x1, gf)


def _moe(h2, e_idx, gate, x1, gf, wgu, bgu, wd, bd, blk, t_tile):
    B, S, D = x1.shape
    T = B * S
    E = wgu.shape[0]
    rank, counts = _ranks(e_idx.reshape(T, TOP_K), t_tile)
    counts = counts[0, :E].astype(I32)
    padded = (counts + blk - 1) // blk * blk
    pad_end = jnp.cumsum(padded)
    pad_start = pad_end - padded
    dest = (pad_start[e_idx.reshape(T, TOP_K)] + rank).reshape(T * TOP_K)
    n_blk = (T * TOP_K + E * (blk - 1) + blk - 1) // blk
    blk_start = jnp.arange(n_blk, dtype=I32) * blk
    blk_e = jnp.minimum(jnp.sum((pad_end[None, :] <= blk_start[:, None]).astype(I32), axis=1), E - 1)
    n_used = (pad_end[-1] // blk).astype(I32).reshape(1)
    xb = _scatter(dest, h2.reshape(T, D), jnp.zeros((n_blk * blk, D), F32), t_tile)
    yb = _experts(blk_e, n_used, xb, wgu, bgu, wd, bd, blk)
    return _combine(dest, yb, gate, x1, gf, t_tile)


def _alibi_slopes(n):
    return jnp.asarray([2.0 ** (-8.0 * (i + 1) / n) for i in range(n)], dtype=F32)


def _lambda_init(layer):
    return 0.8 - 0.6 * math.exp(-0.3 * layer)


def _tile(n, want):
    return want if n % want == 0 else n


def kernel(x_prompt, x_sample, cache_diff_k, cache_diff_v, cache_dsa_k, cache_dsa_v, cache_idx_k, page_table,
           c_prompt, c_sample, w_ada, b_ada, norm_attn_g, w_in, q_norm_diff_g, k_norm_diff_g, q_norm_dsa_g,
           k_norm_dsa_g, diff_lambda_qk, diff_subln_g, w_out, norm_ffn_g, w_router, b_router, w_gate_up,
           b_gate_up, w_down, b_down):
    depth = w_ada.shape[0]
    B, S, D = x_prompt.shape
    DB, DQ, _ = x_sample.shape
    assert DQ == 1, "sample path implements single-token decode"
    n_pool, page = cache_diff_k.shape[1:3]
    n_pg = page_table.shape[1]
    E = w_router.shape[-1]
    d_ff = w_down.shape[2]
    slopes_diff = _alibi_slopes(H_DIFF)
    slopes_dsa = _alibi_slopes(H_DSA)
    pt_flat = page_table.reshape(-1).astype(I32)
    pages_per_step = next(n for n in (16, 8, 4, 2, 1) if n_pg % n == 0)

    pool_dk = jnp.transpose(cache_diff_k, (0, 1, 3, 4, 5, 2)).reshape(depth * n_pool, SEG, page)
    pool_dv = cache_diff_v.reshape(depth * n_pool, page * H_DIFF, 2 * D_HEAD)
    pool_sk = jnp.transpose(cache_dsa_k, (0, 1, 3, 4, 2)).reshape(depth * n_pool, SEG, page)
    pool_sv = jnp.transpose(cache_dsa_v, (0, 1, 3, 4, 2)).reshape(depth * n_pool, SEG, page)
    pool_ik = jnp.transpose(cache_idx_k, (0, 1, 3, 2)).reshape(depth * n_pool, D_IDX, page)

    head_block = (jnp.arange(SEG)[:, None] // D_HEAD) == (jnp.arange(SEG)[None, :] // D_HEAD)
    bd = jnp.where(head_block, 1.0 / D_HEAD, 0.0).astype(BF16)

    xp = x_prompt
    xs = x_sample.reshape(1, DB, D)
    new_p = [[] for _ in range(5)]
    new_s = [[] for _ in range(5)]
    tm = _tile(S, 512)
    for l in range(depth):
        lam_init = _lambda_init(l)
        mod = _ada(jnp.concatenate([c_prompt, c_sample], axis=0), w_ada[l], b_ada[l])
        mod_p = [m.reshape(B, 1, D) for m in jnp.split(mod[:B], 6, axis=-1)]
        mod_s = [m.reshape(1, DB, D) for m in jnp.split(mod[B:], 6, axis=-1)]

        w_pad = jnp.concatenate(
            [w_in[l][:, :OFF_IK + D_IDX], jnp.zeros((D, LANES - D_IDX), F32),
             w_in[l][:, OFF_IK + D_IDX:], jnp.zeros((D, LANES - H_IDX), F32)], axis=1).astype(BF16)
        tile8 = lambda g: jnp.tile(g, SEG // D_HEAD)
        gains = jnp.stack([tile8(q_norm_diff_g[l]), tile8(k_norm_diff_g[l]),
                           tile8(q_norm_dsa_g[l]), tile8(k_norm_dsa_g[l])])
        g_attn = norm_attn_g[l].reshape(1, D)
        lam_p = diff_lambda_qk[l]
        subg = diff_subln_g[l].reshape(1, 2 * D_HEAD)
        wo = w_out[l].astype(BF16)
        wr = jnp.zeros((D, LANES), F32).at[:, :E].set(w_router[l]).astype(BF16)
        br =jnp.zeros((1, LANES), F32).at[0, :E].set(b_router[l])
        g_ffn = norm_ffn_g[l].reshape(1, D)
        wgu = w_gate_up[l].astype(BF16)
        bgu = b_gate_up[l].reshape(E, 1, 2 * d_ff)
        wd = w_down[l].astype(BF16)
        bdn = b_down[l].reshape(E, 1, D)

        (dq, dk, dkb, dv, dvb, sq, sk, skb, sv, svb, iq, ik, ikb, iw) = _inproj(
            xp, mod_p[1], mod_p[0], g_attn, w_pad, bd, gains, tm)
        o_diff = _diff_prompt(slopes_diff, dq, dkb, dvb, lam_p, subg, lam_init, _tile(S, 512), _tile(S, 1024))
        o_dsa = _dsa_prompt(slopes_dsa, sq, iq, iw, skb, svb, ikb, _tile(S, 256), _tile(S, 1024))
        x1, h2, e_idx, gate = _outproj(o_diff, o_dsa, xp, mod_p[2], mod_p[4], mod_p[3], g_ffn,
                                       wo, wr, br, E, tm)
        xp = _moe(h2, e_idx, gate, x1, mod_p[5], wgu, bgu, wd, bdn, 512, _tile(S, 256))
        for j, a in enumerate((dk.reshape(B, S, H_DIFF, 2, D_HEAD), dv.reshape(B, S, H_DIFF, 2 * D_HEAD),
                               sk.reshape(B, S, H_DSA, D_HEAD), sv.reshape(B, S, H_DSA, D_HEAD), ik)):
            new_p[j].append(a)

        (dq, dk, dkb, dv, dvb, sq, sk, skb, sv, svb, iq, ik, ikb, iw) = _inproj(
            xs, mod_s[1], mod_s[0], g_attn, w_pad, bd, gains, DB)
        base = l * n_pool
        scores, score_new = _sample_index(
            pt_flat, iq.reshape(DB, H_IDX, D_IDX), iw.reshape(DB, H_IDX, 1), ik.reshape(DB, 1, D_IDX),
            pool_ik, n_pg, pages_per_step, base)
        bias, newok = _sample_select(scores.reshape(DB, n_pg * page), score_new.reshape(DB, 1),
                                     _tile(n_pg * page, 512))
        row_slope_diff = jnp.repeat(slopes_diff, 2).reshape(2 * H_DIFF, 1)
        o_diff = _paged_attn(
            pt_flat, _block_rows(dq[0], D_HEAD), row_slope_diff, dk.reshape(DB, 1, SEG), dv.reshape(DB, 1, SEG),
            jnp.ones((DB, 1, 1), F32), jnp.zeros((DB, 1, n_pg * page), F32), pool_dk, pool_dv, False,
            (lam_p, subg), functools.partial(_diff_sample_epilogue, lam_init=lam_init),
            n_pg, pages_per_step, base, "diff_sample")
        o_dsa = _paged_attn(
            pt_flat, _block_rows(sq[0], D_HEAD), slopes_dsa.reshape(H_DSA, 1), sk.reshape(DB, 1, SEG),
            sv.reshape(DB, 1, SEG), newok.reshape(DB, 1, 1), bias.reshape(DB, 1, n_pg * page), pool_sk, pool_sv, True,
            (), _dsa_sample_epilogue, n_pg, pages_per_step, base, "dsa_sample")
        x1, h2, e_idx, gate = _outproj(o_diff.reshape(1, DB, SEG).astype(BF16), o_dsa.reshape(1, DB, SEG).astype(BF16),
                                       xs, mod_s[2], mod_s[4], mod_s[3], g_ffn, wo, wr, br, E, DB)
        xs = _moe(h2, e_idx, gate, x1, mod_s[5], wgu, bgu, wd, bdn, 128, DB)
        for j, a in enumerate((dk.reshape(DB, 1, H_DIFF, 2, D_HEAD), dv.reshape(DB, 1, H_DIFF, 2 * D_HEAD),
                               sk.reshape(DB, 1, H_DSA, D_HEAD), sv.reshape(DB, 1, H_DSA, D_HEAD),
                               ik.reshape(DB, 1, D_IDX))):
            new_s[j].append(a)

    return (xp, xs.reshape(DB, 1, D),
            jnp.stack(new_p[0]), jnp.stack(new_p[1]), jnp.stack(new_p[2]), jnp.stack(new_p[3]), jnp.stack(new_p[4]),
            jnp.stack(new_s[0]), jnp.stack(new_s[1]), jnp.stack(new_s[2]), jnp.stack(new_s[3]), jnp.stack(new_s[4]))
```

```python
import functools
import math

import jax
import jax.numpy as jnp
from jax import lax
from jax.experimental import pallas as pl
from jax.experimental.pallas import tpu as pltpu

F32 = jnp.float32
BF16 = jnp.bfloat16
I32 = jnp.int32
I16 = jnp.int16

D_HEAD = 64
H_DIFF = 4
H_DSA = 8
H_IDX = 8
D_IDX = 64
TOPK_KEYS = 256
TOP_K = 4
SWIGLU_LIMIT = 7.0
SWIGLU_ALPHA = 1.702
EPS = 1e-6

LANES = 128
SEG = 512
NEG = -1e30
INT_MIN = -2 ** 31
INT_MAX = 2 ** 31 - 1
VMEM_LIMIT = 56 * 1024 * 1024

OFF_IK = 7 * SEG
OFF_IW = OFF_IK + LANES
N_IN_PAD = OFF_IW + LANES


def _cparams(sem, vmem=VMEM_LIMIT):
    return pltpu.CompilerParams(dimension_semantics=sem, vmem_limit_bytes=vmem)


def _dot(a, b):
    return jnp.dot(a, b, preferred_element_type=F32)


def _dot_nt(a, b):
    return lax.dot_general(a, b, (((1,), (1,)), ((), ())), preferred_element_type=F32)


def _split_bf16(a):
    hi = a.astype(BF16)
    lo = (a - hi.astype(F32)).astype(BF16)
    return hi, lo


def _dot3(a, b):
    ah, al = _split_bf16(a)
    bh, bl = _split_bf16(b)
    return _dot(ah, bh) + _dot(al, bh) + _dot(ah, bl)


def _rms(x):
    return x * lax.rsqrt(jnp.mean(x * x, axis=-1, keepdims=True) + EPS)


def _ada_kernel(c_ref, w_ref, b_ref, o_ref):
    c = c_ref[...]
    o_ref[...] = _dot3(c * jax.nn.sigmoid(c), w_ref[...]) + b_ref[...]


def _ada(c, w, b):
    rows, d = c.shape
    n = w.shape[1]
    tn = 1024 if n % 1024 == 0 else n
    return pl.pallas_call(
        _ada_kernel,
        out_shape=jax.ShapeDtypeStruct((rows, n), F32),
        grid=(n // tn,),
        in_specs=[pl.BlockSpec((rows, d), lambda j: (0, 0)),
                  pl.BlockSpec((d, tn), lambda j: (0, j)),
                  pl.BlockSpec((1, tn), lambda j: (0, j))],
        out_specs=pl.BlockSpec((rows, tn), lambda j: (0, j)),
        compiler_params=_cparams(("arbitrary",)),
        name="ada",
    )(c, w, b.reshape(1, n))


def _inproj_kernel(x_ref, sc_ref, sh_ref, g_ref, w_ref, bd_ref, gains_ref,
                   dq_ref, dk_ref, dkb_ref, dv_ref, dvb_ref,
                   sq_ref, sk_ref, skb_ref, sv_ref, svb_ref,
                   iq_ref, ik_ref, ikb_ref, iw_ref):
    x = x_ref[0]
    h = _rms(x) * g_ref[...] * (1.0 + sc_ref[0]) + sh_ref[0]
    hb = h.astype(BF16)

    def seg(i):
        return _dot(hb, w_ref[:, i * SEG:(i + 1) * SEG])

    def head_norm(y, row):
        ms = _dot((y * y).astype(BF16), bd_ref[...])
        return y * lax.rsqrt(ms + EPS) * gains_ref[row:row + 1, :]

    def with_ones(v):
        vb = v.astype(BF16)
        ones = jnp.ones((vb.shape[0], LANES), BF16)
        parts = []
        for i in range(SEG // LANES):
            parts += [vb[:, i * LANES:(i + 1) * LANES], ones]
        return jnp.concatenate(parts, axis=-1)

    scale = D_HEAD ** -0.5
    dq_ref[0] = (head_norm(seg(0), 0) * scale).astype(BF16)
    dk = head_norm(seg(1), 1)
    dk_ref[0] = dk
    dkb_ref[0] = dk.astype(BF16)
    dv = seg(2)
    dv_ref[0] = dv
    dvb_ref[0] = with_ones(dv)
    sq_ref[0] = (head_norm(seg(3), 2) * scale).astype(BF16)
    sk = head_norm(seg(4), 3)
    sk_ref[0] = sk
    skb_ref[0] = sk.astype(BF16)
    sv = seg(5)
    sv_ref[0] = sv
    svb_ref[0] = with_ones(sv)
    iq_ref[0] = seg(6).astype(BF16)
    ik = _dot(hb, w_ref[:, OFF_IK:OFF_IK + LANES])[:, :D_IDX]
    ik_ref[0] = ik
    ikb_ref[0] = ik.astype(BF16)
    iw_ref[0] = _dot(hb, w_ref[:, OFF_IW:OFF_IW + LANES])[:, :H_IDX] * (H_IDX ** -0.5)


def _inproj(x, sc, sh, g, w_pad, bd, gains, tm):
    B, S, D = x.shape
    mod_rows = sc.shape[1]
    mod_blk = (1, 1, D) if mod_rows == 1 else (1, tm, D)
    mod_map = (lambda b, i: (b, 0, 0)) if mod_rows == 1 else (lambda b, i: (b, i, 0))
    row = lambda w: pl.BlockSpec((1, tm, w), lambda b, i: (b, i, 0))
    const = lambda shp: pl.BlockSpec(shp, lambda b, i: (0,) * len(shp))
    sds = lambda w, dt: jax.ShapeDtypeStruct((B, S, w), dt)
    outs = [(SEG, BF16), (SEG, F32), (SEG, BF16), (SEG, F32), (2 * SEG, BF16),
            (SEG, BF16), (SEG, F32), (SEG, BF16), (SEG, F32), (2 * SEG, BF16),
            (SEG, BF16), (D_IDX, F32), (D_IDX, BF16), (H_IDX, F32)]
    return pl.pallas_call(
        _inproj_kernel,
        out_shape=[sds(w, dt) for w, dt in outs],
        grid=(B, S // tm),
        in_specs=[row(D), pl.BlockSpec(mod_blk, mod_map), pl.BlockSpec(mod_blk, mod_map),
                  const((1, D)), const((D, N_IN_PAD)), const((SEG, SEG)), const((4, SEG))],
        out_specs=[row(w) for w, _ in outs],
        compiler_params=_cparams(("parallel", "parallel")),
        name="inproj",
    )(x, sc, sh, g, w_pad, bd, gains)


def _diff_prompt_kernel(slopes_ref, q_ref, k_ref, v_ref, lam_ref, subg_ref, o_ref,
                        m_sc, acc_sc, sa_sc, sb_sc, *, tq, tk, lam_init):
    h = pl.program_id(1)
    q0 = pl.program_id(2) * tq
    slope = slopes_ref[h]
    q = q_ref[0]
    lane = lax.broadcasted_iota(I32, (1, LANES), 1)
    qs = (jnp.where(lane < D_HEAD, q, jnp.zeros_like(q)),
          jnp.where(lane >= D_HEAD, q, jnp.zeros_like(q)))
    rowpos = q0 + lax.broadcasted_iota(I32, (tq, 1), 0)
    col = lax.broadcasted_iota(I32, (1, tk), 1)

    m_sc[...] = jnp.full(m_sc.shape, NEG, F32)
    acc_sc[...] = jnp.zeros(acc_sc.shape, F32)

    n_all = (q0 + tq + tk - 1) // tk

    def scores(c, s_ref):
        start = pl.multiple_of(jnp.minimum(c, n_all - 1) * tk, tk)
        k = k_ref[0, pl.ds(start, tk), :]
        for j in range(2):
            s_ref[j] = _dot_nt(qs[j], k)

    def consume(c, s_ref, masked):
        start = pl.multiple_of(jnp.minimum(c, n_all - 1) * tk, tk)
        v = v_ref[0, pl.ds(start, tk), :]
        colpos = c * tk + col
        bias = slope * (colpos - q0).astype(F32)
        for j in range(2):
            s = s_ref[j] + bias
            if masked:
                s = jnp.where(colpos <= rowpos, s, NEG)
            m_old = m_sc[j]
            m_new = jnp.maximum(m_old, jnp.max(s, axis=-1, keepdims=True))
            alpha = jnp.exp(m_old - m_new)
            p = jnp.exp(s - m_new)
            acc_sc[j] = alpha * acc_sc[j] + _dot(p.astype(BF16), v)
            m_sc[j] = m_new

    def pair(i, masked):
        c = 2 * i
        scores(c + 1, sb_sc)
        consume(c, sa_sc, masked)
        scores(c + 2, sa_sc)
        consume(c + 1, sb_sc, masked)

    def full_body(i, carry):
        pair(i, False)
        return carry

    n_pairs = (n_all + 1) // 2
    scores(0, sa_sc)
    lax.fori_loop(0, n_pairs - 1, full_body, 0)
    pair(n_pairs - 1, True)

    lp = lam_ref[...]
    lam = (jnp.exp(jnp.sum(lp[0:1] * lp[1:2], axis=-1, keepdims=True))
           - jnp.exp(jnp.sum(lp[2:3] * lp[3:4], axis=-1, keepdims=True)) + lam_init)
    o = (acc_sc[0, :, :LANES] / acc_sc[0, :, LANES:]
         - lam * (acc_sc[1, :, :LANES] / acc_sc[1, :, LANES:]))
    o_ref[0] = (_rms(o) * subg_ref[...] * (1.0 - lam_init)).astype(BF16)


def _diff_prompt(slopes, dq, dkb, dvb, lam_p, subg, lam_init, tq, tk):
    B, S, _ = dq.shape
    assert tq == tk, "the causal-mask placement in the kernel assumes square chunks"
    kern = functools.partial(_diff_prompt_kernel, tq=tq, tk=tk, lam_init=lam_init)
    return pl.pallas_call(
        kern,
        out_shape=jax.ShapeDtypeStruct((B, S, SEG), BF16),
        grid=(B, H_DIFF, S // tq),
        in_specs=[pl.BlockSpec(memory_space=pltpu.SMEM),
                  pl.BlockSpec((1, tq, LANES), lambda b, h, i: (b, i, h)),
                  pl.BlockSpec((1, S, LANES), lambda b, h, i: (b, 0, h)),
                  pl.BlockSpec((1, S, 2 * LANES), lambda b, h, i: (b, 0, h)),
                  pl.BlockSpec((4, D_HEAD), lambda b, h, i: (0, 0)),
                  pl.BlockSpec((1, LANES), lambda b, h, i: (0, 0))],
        out_specs=pl.BlockSpec((1, tq, LANES), lambda b, h, i: (b, i, h)),
        scratch_shapes=[pltpu.VMEM((2, tq, 1), F32), pltpu.VMEM((2, tq, 2 * LANES), F32),
                        pltpu.VMEM((2, tq, tk), F32), pltpu.VMEM((2, tq, tk), F32)],
        compiler_params=_cparams(("parallel", "parallel", "parallel")),
        name="diff_prompt",
    )(slopes, dq, dkb, dvb, lam_p, subg)


def _sortable(x):
    x = jnp.where(x == 0.0, jnp.zeros_like(x), x)
    bits = pltpu.bitcast(x, I32)
    return bits ^ ((bits >> 31) & INT_MAX)


def _count(key_sc, n_chunks, tk, rows, pred, extra=None):
    def body(c, acc):
        ind = jnp.where(pred(key_sc[c], c), 1.0, 0.0)
        part = ind[:, 0:LANES]
        for j in range(1, tk // LANES):
            part = part + ind[:, j * LANES:(j + 1) * LANES]
        return acc + part

    acc = lax.fori_loop(0, n_chunks, body, jnp.zeros((rows, LANES), F32))
    cnt = jnp.sum(acc, axis=-1, keepdims=True)
    if extra is not None:
        cnt = cnt + jnp.where(extra, 1.0, 0.0)
    return cnt


def _kth_largest_halves(key_sc, half_sc, n_chunks, tk, rows, k_sel):
    half_bias = 1 << 15

    def count_ge(cand):
        c16 = cand.astype(I16)

        def body(c, acc):
            ind = jnp.where(half_sc[c] >= c16, jnp.int16(1), jnp.int16(0))
            part = ind[:, 0:LANES]
            for j in range(1, tk // LANES):
                part = part + ind[:, j * LANES:(j + 1) * LANES]
            return acc + part.astype(F32)

        acc = lax.fori_loop(0, n_chunks, body, jnp.zeros((rows, LANES), F32))
        return jnp.sum(acc, axis=-1, keepdims=True)

    def search(k_need):
        def bit_body(i, u):
            cand = u | lax.shift_left(jnp.int32(1), 15 - i)
            return jnp.where(count_ge(cand - half_bias) >= k_need, cand, u)

        return lax.fori_loop(0, 16, bit_body, jnp.zeros((rows, 1), I32))

    def fill_hi(c, carry):
        half_sc[c] = (key_sc[c] >> 16).astype(I16)
        return carry

    lax.fori_loop(0, n_chunks, fill_hi, 0)
    kf = jnp.full((rows, 1), float(k_sel), F32)
    hi = search(kf) - half_bias
    above = jnp.where(hi + 1 < half_bias, count_ge(jnp.minimum(hi + 1, half_bias - 1)), 0.0)

    def fill_lo(c, carry):
        kc = key_sc[c]
        lo = (kc & (2 * half_bias - 1)) - half_bias
        half_sc[c] = jnp.where((kc >> 16) == hi, lo, -half_bias).astype(I16)
        return carry

    lax.fori_loop(0, n_chunks, fill_lo, 0)
    return lax.shift_left(hi, 16) | search(kf - above)


def _topk_select(key_sc, n_chunks, tk, rows, k_sel, idx_bits, cut_sc, key_new=None, idx_new=None, half_sc=None):
    col = lax.broadcasted_iota(I32, (1, tk), 1)

    def count(pred, pred_new=None):
        extra = None if key_new is None else pred_new
        return _count(key_sc, n_chunks, tk, rows, pred, extra)

    def count_ge(cand):
        return count(lambda kc, c: kc >= cand, None if key_new is None else key_new >= cand)

    kf = float(k_sel)
    if half_sc is not None:
        assert key_new is None
        thr = _kth_largest_halves(key_sc, half_sc, n_chunks, tk, rows, k_sel)
    else:
        thr = jnp.where(count_ge(jnp.zeros((rows, 1), I32)) >= kf, 0, INT_MIN).astype(I32)

        def bit_body(i, thr):
            cand = thr | lax.shift_left(jnp.int32(1), 30 - i)
            return jnp.where(count_ge(cand) >= kf, cand, thr)

        thr = lax.fori_loop(0, 31, bit_body, thr)

    cnt_gt = count(lambda kc, c: kc > thr, None if key_new is None else key_new > thr)
    cnt_ge = count_ge(thr)
    need = kf - cnt_gt
    tie = (cnt_ge - cnt_gt) > need
    cut_sc[...] = jnp.full((rows, 1), INT_MAX, I32)

    @pl.when(jnp.max(jnp.where(tie, 1.0, 0.0)) > 0.0)
    def _():
        def idx_body(i, cut):
            cand = cut | lax.shift_left(jnp.int32(1), idx_bits - 1 - i)
            cnt = count(lambda kc, c: (kc == thr) & (c * tk + col < cand),
                        None if key_new is None else (key_new == thr) & (idx_new < cand))
            return jnp.where(cnt < need, cand, cut)

        cut = lax.fori_loop(0, idx_bits, idx_body, jnp.zeros((rows, 1), I32))
        cut_sc[...] = jnp.where(tie, cut, INT_MAX)

    return thr, cut_sc[...]


def _dsa_prompt_kernel(slopes_ref, q_ref, iq_ref, iw_ref, k_ref, v_ref, ik_ref, o_ref,
                       key_sc, half_sc, bias_sc, cut_sc, m_sc, acc_sc, *, tq, tk, k_sel, idx_bits):
    q0 = pl.program_id(1) * tq
    n_chunks = (q0 + tq + tk - 1) // tk
    rowpos = q0 + lax.broadcasted_iota(I32, (tq, 1), 0)
    col = lax.broadcasted_iota(I32, (1, tk), 1)

    def idx_body(c, carry):
        start = pl.multiple_of(c * tk, tk)
        ikc = ik_ref[0, pl.ds(start, tk), :]
        acc = jnp.zeros((tq, tk), F32)
        for h in range(H_IDX):
            r = jnp.maximum(_dot_nt(iq_ref[0, :, h * D_IDX:(h + 1) * D_IDX], ikc), 0.0)
            acc = acc + r * iw_ref[0, :, h:h + 1]
        acc = jnp.where(start + col <= rowpos, acc, -jnp.inf)
        key_sc[c] = _sortable(acc)
        return carry

    lax.fori_loop(0, n_chunks, idx_body, 0)

    thr, cut = _topk_select(key_sc, n_chunks, tk, tq, k_sel, idx_bits, cut_sc, half_sc=half_sc)

    def mask_body(c, carry):
        kc = key_sc[c]
        idx = c * tk + col
        sel = (kc > thr) | ((kc == thr) & (idx <= cut))
        bias_sc[c] = jnp.where(sel & (idx <= rowpos), 0.0, NEG)
        return carry

    lax.fori_loop(0, n_chunks, mask_body, 0)

    m_sc[...] = jnp.full(m_sc.shape, NEG, F32)
    acc_sc[...] = jnp.zeros(acc_sc.shape, F32)
    lane = lax.broadcasted_iota(I32, (1, LANES), 1)
    is_lo = lane < D_HEAD

    def att_body(c, carry):
        start = pl.multiple_of(c * tk, tk)
        mb = bias_sc[c]
        rel = (start - q0 + col).astype(F32)
        for p in range(H_DSA // 2):
            qp = q_ref[0, :, p * LANES:(p + 1) * LANES]
            kp = k_ref[0, pl.ds(start, tk), p * LANES:(p + 1) * LANES]
            vp = v_ref[0, pl.ds(start, tk), 2 * p * LANES:2 * (p + 1) * LANES]
            for half in range(2):
                h = 2 * p + half
                qh = jnp.where(is_lo if half == 0 else ~is_lo, qp, jnp.zeros_like(qp))
                s = _dot_nt(qh, kp) + (mb + slopes_ref[h] * rel)
                m_old = m_sc[h]
                m_new = jnp.maximum(m_old, jnp.max(s, axis=-1, keepdims=True))
                alpha = jnp.exp(m_old - m_new)
                pr = jnp.exp(s - m_new)
                m_sc[h] = m_new
                acc_sc[h] = alpha * acc_sc[h] + _dot(pr.astype(BF16), vp)
        return carry

    lax.fori_loop(0, n_chunks, att_body, 0)

    for p in range(H_DSA // 2):
        lo = acc_sc[2 * p, :, :LANES] / acc_sc[2 * p, :, LANES:]
        hi = acc_sc[2 * p + 1, :, :LANES] / acc_sc[2 * p + 1, :, LANES:]
        o_ref[0, :, p * LANES:(p + 1) * LANES] = jnp.where(is_lo, lo, hi).astype(BF16)


def _dsa_prompt(slopes, sq, iq, iw, skb, svb, ikb, tq, tk):
    B, S, _ = sq.shape
    k_sel = min(TOPK_KEYS, S // 4)
    kern = functools.partial(_dsa_prompt_kernel, tq=tq, tk=tk, k_sel=k_sel,
                             idx_bits=max(1, (S - 1).bit_length()))
    nck = S // tk
    return pl.pallas_call(
        kern,
        out_shape=jax.ShapeDtypeStruct((B, S, SEG), BF16),
        grid=(B, S // tq),
        in_specs=[pl.BlockSpec(memory_space=pltpu.SMEM),
                  pl.BlockSpec((1, tq, SEG), lambda b, i: (b, i, 0)),
                  pl.BlockSpec((1, tq, SEG), lambda b, i: (b, i, 0)),
                  pl.BlockSpec((1, tq, H_IDX), lambda b, i: (b, i, 0)),
                  pl.BlockSpec((1, S, SEG), lambda b, i: (b, 0, 0)),
                  pl.BlockSpec((1, S, 2 * SEG), lambda b, i: (b, 0, 0)),
                  pl.BlockSpec((1, S, D_IDX), lambda b, i: (b, 0, 0))],
        out_specs=pl.BlockSpec((1, tq, SEG), lambda b, i: (b, i, 0)),
        scratch_shapes=[pltpu.VMEM((nck, tq, tk), I32), pltpu.VMEM((nck, tq, tk), I16),
                        pltpu.VMEM((nck, tq, tk), F32),
                        pltpu.VMEM((tq, 1), I32),
                        pltpu.VMEM((H_DSA, tq, 1), F32),
                        pltpu.VMEM((H_DSA, tq, 2 * LANES), F32)],
        compiler_params=_cparams(("parallel", "parallel")),
        name="dsa_prompt",
    )(slopes, sq, iq, iw, skb, svb, ikb)


def _sample_index_kernel(pt_ref, iq_ref, iw_ref, iknew_ref, *refs, n_pages):
    page_refs = refs[:n_pages]
    o_ref, onew_ref = refs[n_pages:]
    iq = iq_ref[0]
    iw = iw_ref[0]
    r = jnp.concatenate([_dot(iq, page_refs[i][0].astype(BF16)) for i in range(n_pages)], axis=-1)
    o_ref[0] = jnp.sum(jnp.maximum(r, 0.0) * iw, axis=0, keepdims=True)

    @pl.when(pl.program_id(1) == 0)
    def _():
        kn = iknew_ref[0].astype(BF16).astype(F32)
        rn = jnp.maximum(jnp.sum(iq.astype(F32) * kn, axis=-1, keepdims=True), 0.0)
        onew_ref[0] = jnp.sum(rn * iw, axis=0, keepdims=True)


def _sample_index(pt_flat, iq, iw, ik_new, pool, n_pg, n_pages, page_base):
    DB = iq.shape[0]
    page = pool.shape[2]
    assert page == LANES
    kern = functools.partial(_sample_index_kernel, n_pages=n_pages)

    def page_spec(i):
        return pl.BlockSpec((1, D_IDX, page),
                            lambda b, g, pt: (page_base + pt[b * n_pg + g * n_pages + i], 0, 0))

    grid_spec = pltpu.PrefetchScalarGridSpec(
        num_scalar_prefetch=1,
        grid=(DB, n_pg // n_pages),
        in_specs=[pl.BlockSpec((1, H_IDX, D_IDX), lambda b, g, pt: (b, 0, 0)),
                  pl.BlockSpec((1, H_IDX, 1), lambda b, g, pt: (b, 0, 0)),
                  pl.BlockSpec((1, 1, D_IDX), lambda b, g, pt: (b, 0, 0))]
                 + [page_spec(i) for i in range(n_pages)],
        out_specs=[pl.BlockSpec((1, 1, n_pages * page), lambda b, g, pt: (b, 0, g)),
                   pl.BlockSpec((1, 1, 1), lambda b, g, pt: (b, 0, 0))],
    )
    return pl.pallas_call(
        kern,
        out_shape=[jax.ShapeDtypeStruct((DB, 1, n_pg * page), F32),
                   jax.ShapeDtypeStruct((DB, 1, 1), F32)],
        grid_spec=grid_spec,
        compiler_params=_cparams(("parallel", "arbitrary")),
        name="sample_index",
    )(pt_flat, iq, iw, ik_new, *([pool] * n_pages))


def _sample_select_kernel(s_ref, snew_ref, bias_ref, newok_ref, key_sc, cut_sc,
                          *, n_chunks, tk, rows, k_sel, idx_bits):
    def to_keys(c, carry):
        key_sc[c] = _sortable(s_ref[c])
        return carry

    lax.fori_loop(0, n_chunks, to_keys, 0)
    key_new = _sortable(snew_ref[...])
    idx_new = jnp.full((rows, 1), n_chunks * tk, I32)
    thr, cut = _topk_select(key_sc, n_chunks, tk, rows, k_sel, idx_bits, cut_sc,
                            key_new=key_new, idx_new=idx_new)
    col = lax.broadcasted_iota(I32, (1, tk), 1)

    def mask_body(c, carry):
        kc = key_sc[c]
        sel = (kc > thr) | ((kc == thr) & (c * tk + col <= cut))
        bias_ref[c] = jnp.where(sel, 0.0, NEG)
        return carry

    lax.fori_loop(0, n_chunks, mask_body, 0)
    sel_new = (key_new > thr) | ((key_new == thr) & (idx_new <= cut))
    newok_ref[...] = jnp.where(sel_new, 1.0, 0.0)


def _sample_select(scores, score_new, tk):
    DB, P = scores.shape
    n_chunks = P // tk
    k_sel = min(TOPK_KEYS, (P + 1) // 4)
    s3 = scores.reshape(DB, n_chunks, tk).transpose(1, 0, 2)
    kern = functools.partial(_sample_select_kernel, n_chunks=n_chunks, tk=tk, rows=DB, k_sel=k_sel,
                             idx_bits=max(1, P.bit_length()))
    bias3, newok = pl.pallas_call(
        kern,
        out_shape=[jax.ShapeDtypeStruct((n_chunks, DB, tk), F32), jax.ShapeDtypeStruct((DB, 1), F32)],
        scratch_shapes=[pltpu.VMEM((n_chunks, DB, tk), I32), pltpu.VMEM((DB, 1), I32)],
        compiler_params=pltpu.CompilerParams(vmem_limit_bytes=VMEM_LIMIT),
        name="sample_select",
    )(s3, score_new)
    return bias3.transpose(1, 0, 2).reshape(DB, P), newok


def _paged_attn_kernel(pt_ref, q_ref, slope_ref, knew_ref, vnew_ref, newok_ref, bias_ref, *refs,
                       n_pages, n_rows, past_len, v_transposed, epilogue):
    k_refs = refs[:n_pages]
    v_refs = refs[n_pages:2 * n_pages]
    extra_refs = refs[2 * n_pages:-4]
    o_ref, m_sc, l_sc, acc_sc = refs[-4:]
    g = pl.program_id(1)
    q = q_ref[0]
    page = LANES

    @pl.when(g == 0)
    def _():
        kn = knew_ref[0].astype(BF16).astype(F32)
        ok = newok_ref[0] > 0.5
        s_new = jnp.sum(q.astype(F32) * kn, axis=-1, keepdims=True)
        m_sc[...] = jnp.where(ok, s_new, NEG)
        l_sc[...] = jnp.where(ok, jnp.ones((n_rows, 1), F32), 0.0)
        vn = vnew_ref[0].astype(BF16).astype(F32)
        acc_sc[...] = jnp.where(ok, jnp.broadcast_to(vn, (n_rows, SEG)), 0.0)

    width = n_pages * page
    pos = g * width + lax.broadcasted_iota(I32, (1, width), 1)
    rel = (pos - past_len).astype(F32)
    s = jnp.concatenate([_dot(q, k_refs[i][0].astype(BF16)) for i in range(n_pages)], axis=-1)
    s = s + slope_ref[...] * rel + bias_ref[0]
    m_old = m_sc[...]
    m_new = jnp.maximum(m_old, jnp.max(s, axis=-1, keepdims=True))
    alpha = jnp.exp(m_old - m_new)
    p = jnp.exp(s - m_new)
    l_sc[...] = alpha * l_sc[...] + jnp.sum(p, axis=-1, keepdims=True)
    p = p.astype(BF16)
    pv = jnp.zeros((n_rows, SEG), F32)
    for i in range(n_pages):
        pi = p[:, i * page:(i + 1) * page]
        if v_transposed:
            pv = pv + _dot_nt(pi, v_refs[i][0].astype(BF16))
        else:
            n_h = v_refs[i].shape[1] // page
            pv = pv + jnp.concatenate(
                [_dot(pi, v_refs[i][0, pl.ds(h, page, stride=n_h), :].astype(BF16)) for h in range(n_h)], axis=-1)
    acc_sc[...] = alpha * acc_sc[...] + pv
    m_sc[...] = m_new

    @pl.when(g == pl.num_programs(1) - 1)
    def _():
        o_ref[0] = epilogue(acc_sc[...] / l_sc[...], *extra_refs)


def _diff_sample_epilogue(on, lam_ref, subg_ref, *, lam_init):
    row = lax.broadcasted_iota(I32, on.shape, 0)
    colh = lax.broadcasted_iota(I32, on.shape, 1) // (2 * D_HEAD)
    own = colh == row // 2
    o1 = jnp.sum(jnp.where(own & (row % 2 == 0), on, 0.0), axis=0, keepdims=True)
    o2 = jnp.sum(jnp.where(own & (row % 2 == 1), on, 0.0), axis=0, keepdims=True)
    lp = lam_ref[...]
    lam = (jnp.exp(jnp.sum(lp[0:1] * lp[1:2], axis=-1, keepdims=True))
           - jnp.exp(jnp.sum(lp[2:3] * lp[3:4], axis=-1, keepdims=True)) + lam_init)
    o = o1 - lam * o2
    w = 2 * D_HEAD
    segs = [_rms(o[:, h * w:(h + 1) * w]) * subg_ref[...] * (1.0 - lam_init) for h in range(H_DIFF)]
    return jnp.concatenate(segs, axis=-1)


def _dsa_sample_epilogue(on):
    row = lax.broadcasted_iota(I32, on.shape, 0)
    colh = lax.broadcasted_iota(I32, on.shape, 1) // D_HEAD
    return jnp.sum(jnp.where(colh == row, on, 0.0), axis=0, keepdims=True)


def _paged_attn(pt_flat, qmat, row_slope, k_new, v_new, newok, bias, k_pool, v_pool, v_transposed, extras,
                epilogue, n_pg, n_pages, page_base, name):
    DB, R, _ = qmat.shape
    page = k_pool.shape[2]
    assert page == LANES
    kern = functools.partial(_paged_attn_kernel, n_pages=n_pages, n_rows=R, past_len=n_pg * page,
                             v_transposed=v_transposed, epilogue=epilogue)

    def page_spec(i, pool):
        blk = (1,) + pool.shape[1:]
        zeros = (0,) * (pool.ndim - 1)
        return pl.BlockSpec(blk, lambda b, g, pt: (page_base + pt[b * n_pg + g * n_pages + i],) + zeros)

    per_b = lambda shp: pl.BlockSpec(shp, lambda b, g, pt: (b, 0, 0))
    grid_spec = pltpu.PrefetchScalarGridSpec(
        num_scalar_prefetch=1,
        grid=(DB, n_pg // n_pages),
        in_specs=[per_b((1, R, SEG)), pl.BlockSpec((R, 1), lambda b, g, pt: (0, 0)),
                  per_b((1, 1, SEG)), per_b((1, 1, SEG)), per_b((1, 1, 1)),
                  pl.BlockSpec((1, 1, n_pages * page), lambda b, g, pt: (b, 0, g))]
                 + [page_spec(i, k_pool) for i in range(n_pages)]
                 + [page_spec(i, v_pool) for i in range(n_pages)]
                 + [pl.BlockSpec(e.shape, lambda b, g, pt: (0, 0)) for e in extras],
        out_specs=per_b((1, 1, SEG)),
        scratch_shapes=[pltpu.VMEM((R, 1), F32), pltpu.VMEM((R, 1), F32), pltpu.VMEM((R, SEG), F32)],
    )
    return pl.pallas_call(
        kern,
        out_shape=jax.ShapeDtypeStruct((DB, 1, SEG), F32),
        grid_spec=grid_spec,
        compiler_params=_cparams(("parallel", "arbitrary")),
        name=name,
    )(pt_flat, qmat, row_slope, k_new, v_new, newok, bias,
      *([k_pool] * n_pages), *([v_pool] * n_pages), *extras)


def _block_rows(q, width):
    n = SEG // width
    own = (jnp.arange(SEG)[None, :] // width) == jnp.arange(n)[:, None]
    return jnp.where(own[None], q[:, None, :], jnp.zeros((), q.dtype))


def _outproj_kernel(od_ref, os_ref, x_ref, ga_ref, scf_ref, shf_ref, g_ref, wo_ref, wr_ref, br_ref,
                    x1_ref, h2_ref, e_ref, gate_ref, *, n_experts):
    o = _dot(od_ref[0], wo_ref[0:SEG, :]) + _dot(os_ref[0], wo_ref[SEG:2 * SEG, :])
    x1 = x_ref[0] + ga_ref[0] * o
    x1_ref[0] = x1
    h2 = _rms(x1) * g_ref[...] * (1.0 + scf_ref[0]) + shf_ref[0]
    h2_ref[0] = h2
    logits = _dot(h2.astype(BF16), wr_ref[...]) + br_ref[...]
    lane = lax.broadcasted_iota(I32, logits.shape, 1).astype(F32)
    lg = jnp.where(lane < n_experts, logits, -jnp.inf)
    vals, idxs = [], []
    for _ in range(TOP_K):
        m = jnp.max(lg, axis=-1, keepdims=True)
        idx = jnp.min(jnp.where(lg == m, lane, float(LANES)), axis=-1, keepdims=True)
        lg = jnp.where(lane == idx, -jnp.inf, lg)
        vals.append(m)
        idxs.append(idx)
    ex = [jnp.exp(v - vals[0]) for v in vals]
    tot = ex[0] + ex[1] + ex[2] + ex[3]
    e_ref[0] = jnp.concatenate(idxs, axis=-1).astype(I32)
    gate_ref[0] = jnp.concatenate([e / tot for e in ex], axis=-1)


def _outproj(od, osx, x, ga, scf, shf, g, wo, wr, br, n_experts, tm):
    B, S, D = x.shape
    mod_rows = ga.shape[1]
    mod_blk = (1, 1, D) if mod_rows == 1 else (1, tm, D)
    mod_map = (lambda b, i: (b, 0, 0)) if mod_rows == 1 else (lambda b, i: (b, i, 0))
    row = lambda w: pl.BlockSpec((1, tm, w), lambda b, i: (b, i, 0))
    const = lambda shp: pl.BlockSpec(shp, lambda b, i: (0,) * len(shp))
    kern = functools.partial(_outproj_kernel, n_experts=n_experts)
    return pl.pallas_call(
        kern,
        out_shape=[jax.ShapeDtypeStruct((B, S, D), F32), jax.ShapeDtypeStruct((B, S, D), F32),
                   jax.ShapeDtypeStruct((B, S, TOP_K), I32), jax.ShapeDtypeStruct((B, S, TOP_K), F32)],
        grid=(B, S // tm),
        in_specs=[row(SEG), row(SEG), row(D)] + [pl.BlockSpec(mod_blk, mod_map)] * 3
                 + [const((1, D)), const((2 * SEG, D)), const((D, LANES)), const((1, LANES))],
        out_specs=[row(D), row(D), row(TOP_K), row(TOP_K)],
        compiler_params=_cparams(("parallel", "parallel")),
        name="outproj_router",
    )(od, osx, x, ga, scf, shf, g, wo, wr, br)


def _rank_kernel(e_ref, rank_ref, cnt_ref, carry_sc, *, tr):
    @pl.when(pl.program_id(0) == 0)
    def _():
        carry_sc[...] = jnp.zeros(carry_sc.shape, F32)

    e = e_ref[...]
    lane = lax.broadcasted_iota(I32, (tr, LANES), 1)
    ohs = [lane == e[:, k:k + 1] for k in range(TOP_K)]
    oh = jnp.zeros((tr, LANES), F32)
    for m in ohs:
        oh = oh + jnp.where(m, 1.0, 0.0)
    r = lax.broadcasted_iota(I32, (tr, tr), 0)
    c = lax.broadcasted_iota(I32, (tr, tr), 1)
    lower = jnp.where(r > c, 1.0, 0.0).astype(BF16)
    rank_full = carry_sc[...] + _dot(lower, oh.astype(BF16))
    cols = [jnp.sum(jnp.where(m, rank_full, 0.0), axis=-1, keepdims=True) for m in ohs]
    rank_ref[...] = jnp.concatenate(cols, axis=-1).astype(I32)
    carry_sc[...] = carry_sc[...] + jnp.sum(oh, axis=0, keepdims=True)
    cnt_ref[...] = carry_sc[...]


def _ranks(e_idx, tr):
    T = e_idx.shape[0]
    kern = functools.partial(_rank_kernel, tr=tr)
    return pl.pallas_call(
        kern,
        out_shape=[jax.ShapeDtypeStruct((T, TOP_K), I32), jax.ShapeDtypeStruct((1, LANES), F32)],
        grid=(T // tr,),
        in_specs=[pl.BlockSpec((tr, TOP_K), lambda i: (i, 0))],
        out_specs=[pl.BlockSpec((tr, TOP_K), lambda i: (i, 0)), pl.BlockSpec((1, LANES), lambda i: (0, 0))],
        scratch_shapes=[pltpu.VMEM((1, LANES), F32)],
        compiler_params=_cparams(("arbitrary",)),
        name="moe_rank",
    )(e_idx)


def _row_copy(src_ref, src_row, dst_ref, dst_row, sem):
    return pltpu.make_async_copy(src_ref.at[pl.ds(src_row, 1)], dst_ref.at[pl.ds(dst_row, 1)], sem)


def _scatter_kernel(dest_ref, h_ref, xb_in_ref, xb_ref, sem, *, ts):
    del xb_in_ref

    def start(r, carry):
        for k in range(TOP_K):
            _row_copy(h_ref, r, xb_ref, dest_ref[r * TOP_K + k], sem).start(priority=k % 2)
        return carry

    def wait(r, carry):
        for k in range(TOP_K):
            _row_copy(h_ref, r, xb_ref, dest_ref[r * TOP_K + k], sem).wait()
        return carry

    lax.fori_loop(0, ts, start, 0)
    lax.fori_loop(0, ts, wait, 0)


def _scatter(dest_flat, h2, xb, ts):
    T, D = h2.shape
    kern = functools.partial(_scatter_kernel, ts=ts)
    return pl.pallas_call(
        kern,
        out_shape=jax.ShapeDtypeStruct(xb.shape, xb.dtype),
        grid=(T // ts,),
        in_specs=[pl.BlockSpec((ts * TOP_K,), lambda i: (i,), memory_space=pltpu.SMEM),
                  pl.BlockSpec((ts, D), lambda i: (i, 0)),
                  pl.BlockSpec(memory_space=pl.ANY)],
        out_specs=pl.BlockSpec(memory_space=pl.ANY),
        scratch_shapes=[pltpu.SemaphoreType.DMA(())],
        input_output_aliases={2: 0},
        compiler_params=pltpu.CompilerParams(dimension_semantics=("arbitrary",), vmem_limit_bytes=VMEM_LIMIT,
                                             has_side_effects=True),
        name="moe_scatter",
    )(dest_flat, h2, xb)


def _expert_kernel(blk_e_ref, n_used_ref, x_ref, wgu_ref, bgu_ref, wd_ref, bd_ref, y_ref, wgu_sc, wd_sc,
                   *, d_ff, chunk):
    j = pl.program_id(0)
    used = j < n_used_ref[0]
    new_expert = (j == 0) | (blk_e_ref[j] != blk_e_ref[jnp.maximum(j - 1, 0)])

    @pl.when(used & new_expert)
    def _():
        wgu_sc[...] = wgu_ref[0].astype(BF16)
        wd_sc[...] = wd_ref[0].astype(BF16)

    @pl.when(used)
    def _():
        x = x_ref[...].astype(BF16)
        acc = jnp.broadcast_to(bd_ref[0], y_ref.shape)
        for c in range(d_ff // chunk):
            lo = c * chunk
            g = _dot(x, wgu_sc[:, lo:lo + chunk]) + bgu_ref[0, :, lo:lo + chunk]
            u = _dot(x, wgu_sc[:, d_ff + lo:d_ff + lo + chunk]) + bgu_ref[0, :, d_ff + lo:d_ff + lo + chunk]
            g = jnp.minimum(g, SWIGLU_LIMIT)
            u = jnp.clip(u, -SWIGLU_LIMIT, SWIGLU_LIMIT)
            a = (u + 1.0) * (g * jax.nn.sigmoid(SWIGLU_ALPHA * g))
            acc = acc + _dot(a.astype(BF16), wd_sc[lo:lo + chunk, :])
        y_ref[...] = acc

    @pl.when(pl.program_id(0) >= n_used_ref[0])
    def _():
        y_ref[...] = jnp.zeros(y_ref.shape, F32)


def _experts(blk_e, n_used, xb, wgu, bgu, wd, bd, blk):
    n_rows, D = xb.shape
    E, _, two_ff = wgu.shape
    d_ff = two_ff // 2
    chunk = 512 if d_ff % 512 == 0 else d_ff
    kern = functools.partial(_expert_kernel, d_ff=d_ff, chunk=chunk)
    grid_spec = pltpu.PrefetchScalarGridSpec(
        num_scalar_prefetch=2,
        grid=(n_rows // blk,),
        in_specs=[pl.BlockSpec((blk, D), lambda j, be, nu: (j, 0)),
                  pl.BlockSpec((1, D, two_ff), lambda j, be, nu: (be[j], 0, 0)),
                  pl.BlockSpec((1, 1, two_ff), lambda j, be, nu: (be[j], 0, 0)),
                  pl.BlockSpec((1, d_ff, D), lambda j, be, nu: (be[j], 0, 0)),
                  pl.BlockSpec((1, 1, D), lambda j, be, nu: (be[j], 0, 0))],
        out_specs=pl.BlockSpec((blk, D), lambda j, be, nu: (j, 0)),
        scratch_shapes=[pltpu.VMEM((D, two_ff), BF16), pltpu.VMEM((d_ff, D), BF16)],
    )
    return pl.pallas_call(
        kern,
        out_shape=jax.ShapeDtypeStruct((n_rows, D), F32),
        grid_spec=grid_spec,
        compiler_params=_cparams(("arbitrary",)),
        name="moe_experts",
    )(blk_e, n_used, xb, wgu, bgu, wd, bd)


def _combine_kernel(dest_ref, yb_ref, gate_ref, x1_ref, gf_ref, o_ref, buf, sem, *, tc):
    def start(r, carry):
        for k in range(TOP_K):
            _row_copy(yb_ref, dest_ref[r * TOP_K + k], buf.at[k], r, sem).start(priority=k % 2)
        return carry

    def wait(r, carry):
        for k in range(TOP_K):
            _row_copy(yb_ref, dest_ref[r * TOP_K + k], buf.at[k], r, sem).wait()
        return carry

    lax.fori_loop(0, tc, start, 0)
    lax.fori_loop(0, tc, wait, 0)
    gate = gate_ref[0]
    y = gate[:, 0:1] * buf[0]
    for k in range(1, TOP_K):
        y = y + gate[:, k:k + 1] * buf[k]
    o_ref[0] = x1_ref[0] + gf_ref[0] * y


def _combine(dest_flat, yb, gate, x1, gf, tc):
    B, S, D = x1.shape
    nt = S // tc
    mod_rows = gf.shape[1]
    mod_blk = (1, 1, D) if mod_rows == 1 else (1, tc, D)
    mod_map = (lambda b, i: (b, 0, 0)) if mod_rows == 1 else (lambda b, i: (b, i, 0))
    kern = functools.partial(_combine_kernel, tc=tc)
    return pl.pallas_call(
        kern,
        out_shape=jax.ShapeDtypeStruct((B, S, D), F32),
        grid=(B, nt),
        in_specs=[pl.BlockSpec((tc * TOP_K,), lambda b, i: (b * nt + i,), memory_space=pltpu.SMEM),
                  pl.BlockSpec(memory_space=pl.ANY),
                  pl.BlockSpec((1, tc, TOP_K), lambda b, i: (b, i, 0)),
                  pl.BlockSpec((1, tc, D), lambda b, i: (b, i, 0)),
                  pl.BlockSpec(mod_blk, mod_map)],
        out_specs=pl.BlockSpec((1, tc, D), lambda b, i: (b, i, 0)),
        scratch_shapes=[pltpu.VMEM((TOP_K, tc, D), F32), pltpu.SemaphoreType.DMA(())],
        compiler_params=_cparams(("arbitrary", "arbitrary")),
        name="moe_combine",
    )(dest_flat, yb, gate, x1, gf)


def _moe(h2, e_idx, gate, x1, gf, wgu, bgu, wd, bd, blk, t_tile):
    B, S, D = x1.shape
    T = B * S
    E = wgu.shape[0]
    rank, counts = _ranks(e_idx.reshape(T, TOP_K), t_tile)
    counts = counts[0, :E].astype(I32)
    padded = (counts + blk - 1) // blk * blk
    pad_end = jnp.cumsum(padded)
    pad_start = pad_end - padded
    dest = (pad_start[e_idx.reshape(T, TOP_K)] + rank).reshape(T * TOP_K)
    n_blk = (T * TOP_K + E * (blk - 1) + blk - 1) // blk
    blk_start = jnp.arange(n_blk, dtype=I32) * blk
    blk_e = jnp.minimum(jnp.sum((pad_end[None, :] <= blk_start[:, None]).astype(I32), axis=1), E - 1)
    n_used = (pad_end[-1] // blk).astype(I32).reshape(1)
    xb = _scatter(dest, h2.reshape(T, D), jnp.zeros((n_blk * blk, D), F32), t_tile)
    yb = _experts(blk_e, n_used, xb, wgu, bgu, wd, bd, blk)
    return _combine(dest, yb, gate, x1, gf, t_tile)


def _alibi_slopes(n):
    return jnp.asarray([2.0 ** (-8.0 * (i + 1) / n) for i in range(n)], dtype=F32)


def _lambda_init(layer):
    return 0.8 - 0.6 * math.exp(-0.3 * layer)


def _tile(n, want):
    return want if n % want == 0 else n


def kernel(x_prompt, x_sample, cache_diff_k, cache_diff_v, cache_dsa_k, cache_dsa_v, cache_idx_k, page_table,
           c_prompt, c_sample, w_ada, b_ada, norm_attn_g, w_in, q_norm_diff_g, k_norm_diff_g, q_norm_dsa_g,
           k_norm_dsa_g, diff_lambda_qk, diff_subln_g, w_out, norm_ffn_g, w_router, b_router, w_gate_up,
           b_gate_up, w_down, b_down):
    depth = w_ada.shape[0]
    B, S, D = x_prompt.shape
    DB, DQ, _ = x_sample.shape
    assert DQ == 1, "sample path implements single-token decode"
    n_pool, page = cache_diff_k.shape[1:3]
    n_pg = page_table.shape[1]
    E = w_router.shape[-1]
    d_ff = w_down.shape[2]
    slopes_diff = _alibi_slopes(H_DIFF)
    slopes_dsa = _alibi_slopes(H_DSA)
    pt_flat = page_table.reshape(-1).astype(I32)
    pages_per_step = next(n for n in (16, 8, 4, 2, 1) if n_pg % n == 0)

    pool_dk = jnp.transpose(cache_diff_k, (0, 1, 3, 4, 5, 2)).reshape(depth * n_pool, SEG, page)
    pool_dv = cache_diff_v.reshape(depth * n_pool, page * H_DIFF, 2 * D_HEAD)
    pool_sk = jnp.transpose(cache_dsa_k, (0, 1, 3, 4, 2)).reshape(depth * n_pool, SEG, page)
    pool_sv = jnp.transpose(cache_dsa_v, (0, 1, 3, 4, 2)).reshape(depth * n_pool, SEG, page)
    pool_ik = jnp.transpose(cache_idx_k, (0, 1, 3, 2)).reshape(depth * n_pool, D_IDX, page)

    head_block = (jnp.arange(SEG)[:, None] // D_HEAD) == (jnp.arange(SEG)[None, :] // D_HEAD)
    bd = jnp.where(head_block, 1.0 / D_HEAD, 0.0).astype(BF16)

    xp = x_prompt
    xs = x_sample.reshape(1, DB, D)
    new_p = [[] for _ in range(5)]
    new_s = [[] for _ in range(5)]
    tm = _tile(S, 512)
    for l in range(depth):
        lam_init = _lambda_init(l)
        mod = _ada(jnp.concatenate([c_prompt, c_sample], axis=0), w_ada[l], b_ada[l])
        mod_p = [m.reshape(B, 1, D) for m in jnp.split(mod[:B], 6, axis=-1)]
        mod_s = [m.reshape(1, DB, D) for m in jnp.split(mod[B:], 6, axis=-1)]

        w_pad = jnp.concatenate(
            [w_in[l][:, :OFF_IK + D_IDX], jnp.zeros((D, LANES - D_IDX), F32),
             w_in[l][:, OFF_IK + D_IDX:], jnp.zeros((D, LANES - H_IDX), F32)], axis=1).astype(BF16)
        tile8 = lambda g: jnp.tile(g, SEG // D_HEAD)
        gains = jnp.stack([tile8(q_norm_diff_g[l]), tile8(k_norm_diff_g[l]),
                           tile8(q_norm_dsa_g[l]), tile8(k_norm_dsa_g[l])])
        g_attn = norm_attn_g[l].reshape(1, D)
        lam_p = diff_lambda_qk[l]
        subg = diff_subln_g[l].reshape(1, 2 * D_HEAD)
        wo = w_out[l].astype(BF16)
        wr = jnp.zeros((D, LANES), F32).at[:, :E].set(w_router[l]).astype(BF16)
        br =jnp.zeros((1, LANES), F32).at[0, :E].set(b_router[l])
        g_ffn = norm_ffn_g[l].reshape(1, D)
        wgu = w_gate_up[l]
        bgu = b_gate_up[l].reshape(E, 1, 2 * d_ff)
        wd = w_down[l]
        bdn = b_down[l].reshape(E, 1, D)

        (dq, dk, dkb, dv, dvb, sq, sk, skb, sv, svb, iq, ik, ikb, iw) = _inproj(
            xp, mod_p[1], mod_p[0], g_attn, w_pad, bd, gains, tm)
        o_diff = _diff_prompt(slopes_diff, dq, dkb, dvb, lam_p, subg, lam_init, _tile(S, 512), _tile(S, 512))
        o_dsa = _dsa_prompt(slopes_dsa, sq, iq, iw, skb, svb, ikb, _tile(S, 256), _tile(S, 1024))
        x1, h2, e_idx, gate = _outproj(o_diff, o_dsa, xp, mod_p[2], mod_p[4], mod_p[3], g_ffn,
                                       wo, wr, br, E, tm)
        xp = _moe(h2, e_idx, gate, x1, mod_p[5], wgu, bgu, wd, bdn, 512, _tile(S, 256))
        for j, a in enumerate((dk.reshape(B, S, H_DIFF, 2, D_HEAD), dv.reshape(B, S, H_DIFF, 2 * D_HEAD),
                               sk.reshape(B, S, H_DSA, D_HEAD), sv.reshape(B, S, H_DSA, D_HEAD), ik)):
            new_p[j].append(a)

        (dq, dk, dkb, dv, dvb, sq, sk, skb, sv, svb, iq, ik, ikb, iw) = _inproj(
            xs, mod_s[1], mod_s[0], g_attn, w_pad, bd, gains, DB)
        base = l * n_pool
        scores, score_new = _sample_index(
            pt_flat, iq.reshape(DB, H_IDX, D_IDX), iw.reshape(DB, H_IDX, 1), ik.reshape(DB, 1, D_IDX),
            pool_ik, n_pg, pages_per_step, base)
        bias, newok = _sample_select(scores.reshape(DB, n_pg * page), score_new.reshape(DB, 1),
                                     _tile(n_pg * page, 512))
        row_slope_diff = jnp.repeat(slopes_diff, 2).reshape(2 * H_DIFF, 1)
        o_diff = _paged_attn(
            pt_flat, _block_rows(dq[0], D_HEAD), row_slope_diff, dk.reshape(DB, 1, SEG), dv.reshape(DB, 1, SEG),
            jnp.ones((DB, 1, 1), F32), jnp.zeros((DB, 1, n_pg * page), F32), pool_dk, pool_dv, False,
            (lam_p, subg), functools.partial(_diff_sample_epilogue, lam_init=lam_init),
            n_pg, pages_per_step, base, "diff_sample")
        o_dsa = _paged_attn(
            pt_flat, _block_rows(sq[0], D_HEAD), slopes_dsa.reshape(H_DSA, 1), sk.reshape(DB, 1, SEG),
            sv.reshape(DB, 1, SEG), newok.reshape(DB, 1, 1), bias.reshape(DB, 1, n_pg * page), pool_sk, pool_sv, True,
            (), _dsa_sample_epilogue, n_pg, pages_per_step, base, "dsa_sample")
        x1, h2, e_idx, gate = _outproj(o_diff.reshape(1, DB, SEG).astype(BF16), o_dsa.reshape(1, DB, SEG).astype(BF16),
                                       xs, mod_s[2], mod_s[4], mod_s[3], g_ffn, wo, wr, br, E, DB)
        xs = _moe(h2, e_idx, gate, x1, mod_s[5], wgu, bgu, wd, bdn, 128, DB)
        for j, a in enumerate((dk.reshape(DB, 1, H_DIFF, 2, D_HEAD), dv.reshape(DB, 1, H_DIFF, 2 * D_HEAD),
                               sk.reshape(DB, 1, H_DSA, D_HEAD), sv.reshape(DB, 1, H_DSA, D_HEAD),
                               ik.reshape(DB, 1, D_IDX))):
            new_s[j].append(a)

    return (xp, xs.reshape(DB, 1, D),
            jnp.stack(new_p[0]), jnp.stack(new_p[1]), jnp.stack(new_p[2]), jnp.stack(new_p[3]), jnp.stack(new_p[4]),
            jnp.stack(new_s[0]), jnp.stack(new_s[1]), jnp.stack(new_s[2]), jnp.stack(new_s[3]), jnp.stack(new_s[4]))
```

```python
import functools
import math

import jax
import jax.numpy as jnp
from jax import lax
from jax.experimental import pallas as pl
from jax.experimental.pallas import tpu as pltpu

F32 = jnp.float32
BF16 = jnp.bfloat16
I32 = jnp.int32
I16 = jnp.int16

D_HEAD = 64
H_DIFF = 4
H_DSA = 8
H_IDX = 8
D_IDX = 64
TOPK_KEYS = 256
TOP_K = 4
SWIGLU_LIMIT = 7.0
SWIGLU_ALPHA = 1.702
EPS = 1e-6

LANES = 128
SEG = 512
NEG = -1e30
INT_MIN = -2 ** 31
INT_MAX = 2 ** 31 - 1
VMEM_LIMIT = 56 * 1024 * 1024

OFF_IK = 7 * SEG
OFF_IW = OFF_IK + LANES
N_IN_PAD = OFF_IW + LANES


def _cparams(sem, vmem=VMEM_LIMIT):
    return pltpu.CompilerParams(dimension_semantics=sem, vmem_limit_bytes=vmem)


def _dot(a, b):
    return jnp.dot(a, b, preferred_element_type=F32)


def _dot_nt(a, b):
    return lax.dot_general(a, b, (((1,), (1,)), ((), ())), preferred_element_type=F32)


def _rms(x):
    return x * lax.rsqrt(jnp.mean(x * x, axis=-1, keepdims=True) + EPS)


def _ada_kernel(c_ref, w_ref, b_ref, o_ref):
    c = c_ref[...]
    o_ref[...] = _dot((c * jax.nn.sigmoid(c)).astype(BF16), w_ref[...].astype(BF16)) + b_ref[...]


def _ada(c, w, b):
    rows, d = c.shape
    n = w.shape[1]
    tn = 1024 if n % 1024 == 0 else n
    return pl.pallas_call(
        _ada_kernel,
        out_shape=jax.ShapeDtypeStruct((rows, n), F32),
        grid=(n // tn,),
        in_specs=[pl.BlockSpec((rows, d), lambda j: (0, 0)),
                  pl.BlockSpec((d, tn), lambda j: (0, j)),
                  pl.BlockSpec((1, tn), lambda j: (0, j))],
        out_specs=pl.BlockSpec((rows, tn), lambda j: (0, j)),
        compiler_params=_cparams(("arbitrary",)),
        name="ada",
    )(c, w, b.reshape(1, n))


def _inproj_kernel(x_ref, sc_ref, sh_ref, g_ref, w_ref, bd_ref, gains_ref,
                   dq_ref, dk_ref, dkb_ref, dv_ref, dvb_ref,
                   sq_ref, sk_ref, skb_ref, sv_ref, svb_ref,
                   iq_ref, ik_ref, ikb_ref, iw_ref):
    x = x_ref[0]
    h = _rms(x) * g_ref[...] * (1.0 + sc_ref[0]) + sh_ref[0]
    hb = h.astype(BF16)

    def seg(i):
        return _dot(hb, w_ref[:, i * SEG:(i + 1) * SEG])

    def head_norm(y, row):
        ms = _dot((y * y).astype(BF16), bd_ref[...])
        return y * lax.rsqrt(ms + EPS) * gains_ref[row:row + 1, :]

    def with_ones(v):
        vb = v.astype(BF16)
        ones = jnp.ones((vb.shape[0], LANES), BF16)
        parts = []
        for i in range(SEG // LANES):
            parts += [vb[:, i * LANES:(i + 1) * LANES], ones]
        return jnp.concatenate(parts, axis=-1)

    scale = D_HEAD ** -0.5
    dq_ref[0] = (head_norm(seg(0), 0) * scale).astype(BF16)
    dk = head_norm(seg(1), 1)
    dk_ref[0] = dk
    dkb_ref[0] = dk.astype(BF16)
    dv = seg(2)
    dv_ref[0] = dv
    dvb_ref[0] = with_ones(dv)
    sq_ref[0] = (head_norm(seg(3), 2) * scale).astype(BF16)
    sk = head_norm(seg(4), 3)
    sk_ref[0] = sk
    skb_ref[0] = sk.astype(BF16)
    sv = seg(5)
    sv_ref[0] = sv
    svb_ref[0] = with_ones(sv)
    iq_ref[0] = seg(6).astype(BF16)
    ik = _dot(hb, w_ref[:, OFF_IK:OFF_IK + LANES])[:, :D_IDX]
    ik_ref[0] = ik
    ikb_ref[0] = ik.astype(BF16)
    iw_ref[0] = _dot(hb, w_ref[:, OFF_IW:OFF_IW + LANES])[:, :H_IDX] * (H_IDX ** -0.5)


def _inproj(x, sc, sh, g, w_pad, bd, gains, tm):
    B, S, D = x.shape
    mod_rows = sc.shape[1]
    mod_blk = (1, 1, D) if mod_rows == 1 else (1, tm, D)
    mod_map = (lambda b, i: (b, 0, 0)) if mod_rows == 1 else (lambda b, i: (b, i, 0))
    row = lambda w: pl.BlockSpec((1, tm, w), lambda b, i: (b, i, 0))
    const = lambda shp: pl.BlockSpec(shp, lambda b, i: (0,) * len(shp))
    sds = lambda w, dt: jax.ShapeDtypeStruct((B, S, w), dt)
    outs = [(SEG, BF16), (SEG, F32), (SEG, BF16), (SEG, F32), (2 * SEG, BF16),
            (SEG, BF16), (SEG, F32), (SEG, BF16), (SEG, F32), (2 * SEG, BF16),
            (SEG, BF16), (D_IDX, F32), (D_IDX, BF16), (H_IDX, F32)]
    return pl.pallas_call(
        _inproj_kernel,
        out_shape=[sds(w, dt) for w, dt in outs],
        grid=(B, S // tm),
        in_specs=[row(D), pl.BlockSpec(mod_blk, mod_map), pl.BlockSpec(mod_blk, mod_map),
                  const((1, D)), const((D, N_IN_PAD)), const((SEG, SEG)), const((4, SEG))],
        out_specs=[row(w) for w, _ in outs],
        compiler_params=_cparams(("parallel", "parallel")),
        name="inproj",
    )(x, sc, sh, g, w_pad, bd, gains)


def _diff_prompt_kernel(slopes_ref, q_ref, k_ref, v_ref, lam_ref, subg_ref, o_ref,
                        m_sc, acc_sc, sa_sc, sb_sc, *, tq, tk, lam_init):
    h = pl.program_id(1)
    q0 = pl.program_id(2) * tq
    slope = slopes_ref[h]
    q = q_ref[0]
    lane = lax.broadcasted_iota(I32, (1, LANES), 1)
    qs = (jnp.where(lane < D_HEAD, q, jnp.zeros_like(q)),
          jnp.where(lane >= D_HEAD, q, jnp.zeros_like(q)))
    rowpos = q0 + lax.broadcasted_iota(I32, (tq, 1), 0)
    col = lax.broadcasted_iota(I32, (1, tk), 1)

    m_sc[...] = jnp.full(m_sc.shape, NEG, F32)
    acc_sc[...] = jnp.zeros(acc_sc.shape, F32)

    n_all = (q0 + tq + tk - 1) // tk

    def scores(c, s_ref):
        start = pl.multiple_of(jnp.minimum(c, n_all - 1) * tk, tk)
        k = k_ref[0, pl.ds(start, tk), :]
        for j in range(2):
            s_ref[j] = _dot_nt(qs[j], k)

    def consume(c, s_ref, masked):
        start = pl.multiple_of(jnp.minimum(c, n_all - 1) * tk, tk)
        v = v_ref[0, pl.ds(start, tk), :]
        colpos = c * tk + col
        bias = slope * (colpos - q0).astype(F32)
        for j in range(2):
            s = s_ref[j] + bias
            if masked:
                s = jnp.where(colpos <= rowpos, s, NEG)
            m_old = m_sc[j]
            m_new = jnp.maximum(m_old, jnp.max(s, axis=-1, keepdims=True))
            alpha = jnp.exp(m_old - m_new)
            p = jnp.exp(s - m_new)
            acc_sc[j] = alpha * acc_sc[j] + _dot(p.astype(BF16), v)
            m_sc[j] = m_new

    def pair(i, masked):
        c = 2 * i
        scores(c + 1, sb_sc)
        consume(c, sa_sc, masked)
        scores(c + 2, sa_sc)
        consume(c + 1, sb_sc, masked)

    def full_body(i, carry):
        pair(i, False)
        return carry

    n_pairs = (n_all + 1) // 2
    scores(0, sa_sc)
    lax.fori_loop(0, n_pairs - 1, full_body, 0)
    pair(n_pairs - 1, True)

    lp = lam_ref[...]
    lam = (jnp.exp(jnp.sum(lp[0:1] * lp[1:2], axis=-1, keepdims=True))
           - jnp.exp(jnp.sum(lp[2:3] * lp[3:4], axis=-1, keepdims=True)) + lam_init)
    o = (acc_sc[0, :, :LANES] / acc_sc[0, :, LANES:]
         - lam * (acc_sc[1, :, :LANES] / acc_sc[1, :, LANES:]))
    o_ref[0] = (_rms(o) * subg_ref[...] * (1.0 - lam_init)).astype(BF16)


def _diff_prompt(slopes, dq, dkb, dvb, lam_p, subg, lam_init, tq, tk):
    B, S, _ = dq.shape
    assert tq == tk, "the causal-mask placement in the kernel assumes square chunks"
    kern = functools.partial(_diff_prompt_kernel, tq=tq, tk=tk, lam_init=lam_init)
    return pl.pallas_call(
        kern,
        out_shape=jax.ShapeDtypeStruct((B, S, SEG), BF16),
        grid=(B, H_DIFF, S // tq),
        in_specs=[pl.BlockSpec(memory_space=pltpu.SMEM),
                  pl.BlockSpec((1, tq, LANES), lambda b, h, i: (b, i, h)),
                  pl.BlockSpec((1, S, LANES), lambda b, h, i: (b, 0, h)),
                  pl.BlockSpec((1, S, 2 * LANES), lambda b, h, i: (b, 0, h)),
                  pl.BlockSpec((4, D_HEAD), lambda b, h, i: (0, 0)),
                  pl.BlockSpec((1, LANES), lambda b, h, i: (0, 0))],
        out_specs=pl.BlockSpec((1, tq, LANES), lambda b, h, i: (b, i, h)),
        scratch_shapes=[pltpu.VMEM((2, tq, 1), F32), pltpu.VMEM((2, tq, 2 * LANES), F32),
                        pltpu.VMEM((2, tq, tk), F32), pltpu.VMEM((2, tq, tk), F32)],
        compiler_params=_cparams(("parallel", "parallel", "parallel")),
        name="diff_prompt",
    )(slopes, dq, dkb, dvb, lam_p, subg)


def _sortable(x):
    x = jnp.where(x == 0.0, jnp.zeros_like(x), x)
    bits = pltpu.bitcast(x, I32)
    return bits ^ ((bits >> 31) & INT_MAX)


def _count(key_sc, n_chunks, tk, rows, pred, extra=None):
    def body(c, acc):
        ind = jnp.where(pred(key_sc[c], c), 1.0, 0.0)
        part = ind[:, 0:LANES]
        for j in range(1, tk // LANES):
            part = part + ind[:, j * LANES:(j + 1) * LANES]
        return acc + part

    acc = lax.fori_loop(0, n_chunks, body, jnp.zeros((rows, LANES), F32))
    cnt = jnp.sum(acc, axis=-1, keepdims=True)
    if extra is not None:
        cnt = cnt + jnp.where(extra, 1.0, 0.0)
    return cnt


def _kth_largest_halves(key_sc, half_sc, n_chunks, tk, rows, k_sel):
    half_bias = 1 << 15

    def count_ge(cand):
        c16 = cand.astype(I16)

        def body(c, acc):
            ind = jnp.where(half_sc[c] >= c16, jnp.int16(1), jnp.int16(0))
            part = ind[:, 0:LANES]
            for j in range(1, tk // LANES):
                part = part + ind[:, j * LANES:(j + 1) * LANES]
            return acc + part

        acc = lax.fori_loop(0, n_chunks, body, jnp.zeros((rows, LANES), I16))
        return jnp.sum(acc.astype(F32), axis=-1, keepdims=True)

    def search(k_need):
        def bit_body(i, u):
            cand = u | lax.shift_left(jnp.int32(1), 15 - i)
            return jnp.where(count_ge(cand - half_bias) >= k_need, cand, u)

        return lax.fori_loop(0, 16, bit_body, jnp.zeros((rows, 1), I32))

    def fill_hi(c, carry):
        half_sc[c] = (key_sc[c] >> 16).astype(I16)
        return carry

    lax.fori_loop(0, n_chunks, fill_hi, 0)
    kf = jnp.full((rows, 1), float(k_sel), F32)
    hi = search(kf) - half_bias
    above = jnp.where(hi + 1 < half_bias, count_ge(jnp.minimum(hi + 1, half_bias - 1)), 0.0)

    def fill_lo(c, carry):
        kc = key_sc[c]
        lo = (kc & (2 * half_bias - 1)) - half_bias
        half_sc[c] = jnp.where((kc >> 16) == hi, lo, -half_bias).astype(I16)
        return carry

    lax.fori_loop(0, n_chunks, fill_lo, 0)
    return lax.shift_left(hi, 16) | search(kf - above)


def _topk_select(key_sc, n_chunks, tk, rows, k_sel, idx_bits, cut_sc, key_new=None, idx_new=None, half_sc=None):
    col = lax.broadcasted_iota(I32, (1, tk), 1)

    def count(pred, pred_new=None):
        extra = None if key_new is None else pred_new
        return _count(key_sc, n_chunks, tk, rows, pred, extra)

    def count_ge(cand):
        return count(lambda kc, c: kc >= cand, None if key_new is None else key_new >= cand)

    kf = float(k_sel)
    if half_sc is not None:
        assert key_new is None
        thr = _kth_largest_halves(key_sc, half_sc, n_chunks, tk, rows, k_sel)
    else:
        thr = jnp.where(count_ge(jnp.zeros((rows, 1), I32)) >= kf, 0, INT_MIN).astype(I32)

        def bit_body(i, thr):
            cand = thr | lax.shift_left(jnp.int32(1), 30 - i)
            return jnp.where(count_ge(cand) >= kf, cand, thr)

        thr = lax.fori_loop(0, 31, bit_body, thr)

    cnt_gt = count(lambda kc, c: kc > thr, None if key_new is None else key_new > thr)
    cnt_ge = count_ge(thr)
    need = kf - cnt_gt
    tie = (cnt_ge - cnt_gt) > need
    cut_sc[...] = jnp.full((rows, 1), INT_MAX, I32)

    @pl.when(jnp.max(jnp.where(tie, 1.0, 0.0)) > 0.0)
    def _():
        def idx_body(i, cut):
            cand = cut | lax.shift_left(jnp.int32(1), idx_bits - 1 - i)
            cnt = count(lambda kc, c: (kc == thr) & (c * tk + col < cand),
                        None if key_new is None else (key_new == thr) & (idx_new < cand))
            return jnp.where(cnt < need, cand, cut)

        cut = lax.fori_loop(0, idx_bits, idx_body, jnp.zeros((rows, 1), I32))
        cut_sc[...] = jnp.where(tie, cut, INT_MAX)

    return thr, cut_sc[...]


def _dsa_prompt_kernel(slopes_ref, q_ref, iq_ref, iw_ref, k_ref, v_ref, ik_ref, o_ref,
                       key_sc, half_sc, cut_sc, m_sc, acc_sc, *, tq, tk, k_sel, idx_bits):
    q0 = pl.program_id(1) * tq
    n_chunks = (q0 + tq + tk - 1) // tk
    rowpos = q0 + lax.broadcasted_iota(I32, (tq, 1), 0)
    col = lax.broadcasted_iota(I32, (1, tk), 1)

    def idx_body(c, carry):
        start = pl.multiple_of(c * tk, tk)
        ikc = ik_ref[0, pl.ds(start, tk), :]
        acc = jnp.zeros((tq, tk), F32)
        for h in range(H_IDX):
            r = jnp.maximum(_dot_nt(iq_ref[0, :, h * D_IDX:(h + 1) * D_IDX], ikc), 0.0)
            acc = acc + r * iw_ref[0, :, h:h + 1]
        acc = jnp.where(start + col <= rowpos, acc, -jnp.inf)
        key_sc[c] = _sortable(acc)
        return carry

    lax.fori_loop(0, n_chunks, idx_body, 0)

    thr, cut = _topk_select(key_sc, n_chunks, tk, tq, k_sel, idx_bits, cut_sc, half_sc=half_sc)

    m_sc[...] = jnp.full(m_sc.shape, NEG, F32)
    acc_sc[...] = jnp.zeros(acc_sc.shape, F32)
    lane = lax.broadcasted_iota(I32, (1, LANES), 1)
    is_lo = lane < D_HEAD

    def att_body(c, carry):
        start = pl.multiple_of(c * tk, tk)
        kc = key_sc[c]
        idx = start + col
        sel = (kc > thr) | ((kc == thr) & (idx <= cut))
        mb = jnp.where(sel & (idx <= rowpos), 0.0, NEG)
        rel = (start - q0 + col).astype(F32)
        for p in range(H_DSA // 2):
            qp = q_ref[0, :, p * LANES:(p + 1) * LANES]
            kp = k_ref[0, pl.ds(start, tk), p * LANES:(p + 1) * LANES]
            vp = v_ref[0, pl.ds(start, tk), 2 * p * LANES:2 * (p + 1) * LANES]
            for half in range(2):
                h = 2 * p + half
                qh = jnp.where(is_lo if half == 0 else ~is_lo, qp, jnp.zeros_like(qp))
                s = _dot_nt(qh, kp) + (mb + slopes_ref[h] * rel)
                m_old = m_sc[h]
                m_new = jnp.maximum(m_old, jnp.max(s, axis=-1, keepdims=True))
                alpha = jnp.exp(m_old - m_new)
                pr = jnp.exp(s - m_new)
                m_sc[h] = m_new
                acc_sc[h] = alpha * acc_sc[h] + _dot(pr.astype(BF16), vp)
        return carry

    lax.fori_loop(0, n_chunks, att_body, 0)

    for p in range(H_DSA // 2):
        lo = acc_sc[2 * p, :, :LANES] / acc_sc[2 * p, :, LANES:]
        hi = acc_sc[2 * p + 1, :, :LANES] / acc_sc[2 * p + 1, :, LANES:]
        o_ref[0, :, p * LANES:(p + 1) * LANES] = jnp.where(is_lo, lo, hi).astype(BF16)


def _dsa_prompt(slopes, sq, iq, iw, skb, svb, ikb, tq, tk):
    B, S, _ = sq.shape
    k_sel = min(TOPK_KEYS, S // 4)
    kern = functools.partial(_dsa_prompt_kernel, tq=tq, tk=tk, k_sel=k_sel,
                             idx_bits=max(1, (S - 1).bit_length()))
    nck = S // tk
    return pl.pallas_call(
        kern,
        out_shape=jax.ShapeDtypeStruct((B, S, SEG), BF16),
        grid=(B, S // tq),
        in_specs=[pl.BlockSpec(memory_space=pltpu.SMEM),
                  pl.BlockSpec((1, tq, SEG), lambda b, i: (b, i, 0)),
                  pl.BlockSpec((1, tq, SEG), lambda b, i: (b, i, 0)),
                  pl.BlockSpec((1, tq, H_IDX), lambda b, i: (b, i, 0)),
                  pl.BlockSpec((1, S, SEG), lambda b, i: (b, 0, 0)),
                  pl.BlockSpec((1, S, 2 * SEG), lambda b, i: (b, 0, 0)),
                  pl.BlockSpec((1, S, D_IDX), lambda b, i: (b, 0, 0))],
        out_specs=pl.BlockSpec((1, tq, SEG), lambda b, i: (b, i, 0)),
        scratch_shapes=[pltpu.VMEM((nck, tq, tk), I32), pltpu.VMEM((nck, tq, tk), I16),
                        pltpu.VMEM((tq, 1), I32),
                        pltpu.VMEM((H_DSA, tq, 1), F32),
                        pltpu.VMEM((H_DSA, tq, 2 * LANES), F32)],
        compiler_params=_cparams(("parallel", "parallel")),
        name="dsa_prompt",
    )(slopes, sq, iq, iw, skb, svb, ikb)


def _sample_index_kernel(pt_ref, iq_ref, iw_ref, iknew_ref, *refs, n_pages):
    page_refs = refs[:n_pages]
    o_ref, onew_ref = refs[n_pages:]
    iq = iq_ref[0]
    iw = iw_ref[0]
    r = jnp.concatenate([_dot(iq, page_refs[i][0].astype(BF16)) for i in range(n_pages)], axis=-1)
    o_ref[0] = jnp.sum(jnp.maximum(r, 0.0) * iw, axis=0, keepdims=True)

    @pl.when(pl.program_id(1) == 0)
    def _():
        kn = iknew_ref[0].astype(BF16).astype(F32)
        rn = jnp.maximum(jnp.sum(iq.astype(F32) * kn, axis=-1, keepdims=True), 0.0)
        onew_ref[0] = jnp.sum(rn * iw, axis=0, keepdims=True)


def _sample_index(pt_flat, iq, iw, ik_new, pool, n_pg, n_pages, page_base):
    DB = iq.shape[0]
    page = pool.shape[2]
    assert page == LANES
    kern = functools.partial(_sample_index_kernel, n_pages=n_pages)

    def page_spec(i):
        return pl.BlockSpec((1, D_IDX, page),
                            lambda b, g, pt: (page_base + pt[b * n_pg + g * n_pages + i], 0, 0))

    grid_spec = pltpu.PrefetchScalarGridSpec(
        num_scalar_prefetch=1,
        grid=(DB, n_pg // n_pages),
        in_specs=[pl.BlockSpec((1, H_IDX, D_IDX), lambda b, g, pt: (b, 0, 0)),
                  pl.BlockSpec((1, H_IDX, 1), lambda b, g, pt: (b, 0, 0)),
                  pl.BlockSpec((1, 1, D_IDX), lambda b, g, pt: (b, 0, 0))]
                 + [page_spec(i) for i in range(n_pages)],
        out_specs=[pl.BlockSpec((1, 1, n_pages * page), lambda b, g, pt: (b, 0, g)),
                   pl.BlockSpec((1, 1, 1), lambda b, g, pt: (b, 0, 0))],
    )
    return pl.pallas_call(
        kern,
        out_shape=[jax.ShapeDtypeStruct((DB, 1, n_pg * page), F32),
                   jax.ShapeDtypeStruct((DB, 1, 1), F32)],
        grid_spec=grid_spec,
        compiler_params=_cparams(("parallel", "arbitrary")),
        name="sample_index",
    )(pt_flat, iq, iw, ik_new, *([pool] * n_pages))


def _sample_select_kernel(s_ref, snew_ref, bias_ref, newok_ref, key_sc, cut_sc,
                          *, n_chunks, tk, rows, k_sel, idx_bits):
    def to_keys(c, carry):
        key_sc[c] = _sortable(s_ref[c])
        return carry

    lax.fori_loop(0, n_chunks, to_keys, 0)
    key_new = _sortable(snew_ref[...])
    idx_new = jnp.full((rows, 1), n_chunks * tk, I32)
    thr, cut = _topk_select(key_sc, n_chunks, tk, rows, k_sel, idx_bits, cut_sc,
                            key_new=key_new, idx_new=idx_new)
    col = lax.broadcasted_iota(I32, (1, tk), 1)

    def mask_body(c, carry):
        kc = key_sc[c]
        sel = (kc > thr) | ((kc == thr) & (c * tk + col <= cut))
        bias_ref[c] = jnp.where(sel, 0.0, NEG)
        return carry

    lax.fori_loop(0, n_chunks, mask_body, 0)
    sel_new = (key_new > thr) | ((key_new == thr) & (idx_new <= cut))
    newok_ref[...] = jnp.where(sel_new, 1.0, 0.0)


def _sample_select(scores, score_new, tk):
    DB, P = scores.shape
    n_chunks = P // tk
    k_sel = min(TOPK_KEYS, (P + 1) // 4)
    s3 = scores.reshape(DB, n_chunks, tk).transpose(1, 0, 2)
    kern = functools.partial(_sample_select_kernel, n_chunks=n_chunks, tk=tk, rows=DB, k_sel=k_sel,
                             idx_bits=max(1, P.bit_length()))
    bias3, newok = pl.pallas_call(
        kern,
        out_shape=[jax.ShapeDtypeStruct((n_chunks, DB, tk), F32), jax.ShapeDtypeStruct((DB, 1), F32)],
        scratch_shapes=[pltpu.VMEM((n_chunks, DB, tk), I32), pltpu.VMEM((DB, 1), I32)],
        compiler_params=pltpu.CompilerParams(vmem_limit_bytes=VMEM_LIMIT),
        name="sample_select",
    )(s3, score_new)
    return bias3.transpose(1, 0, 2).reshape(DB, P), newok


def _paged_attn_kernel(pt_ref, q_ref, slope_ref, knew_ref, vnew_ref, newok_ref, bias_ref, *refs,
                       n_pages, n_rows, past_len, v_transposed, epilogue):
    k_refs = refs[:n_pages]
    v_refs = refs[n_pages:2 * n_pages]
    extra_refs = refs[2 * n_pages:-4]
    o_ref, m_sc, l_sc, acc_sc = refs[-4:]
    g = pl.program_id(1)
    q = q_ref[0]
    page = LANES

    @pl.when(g == 0)
    def _():
        kn = knew_ref[0].astype(BF16).astype(F32)
        ok = newok_ref[0] > 0.5
        s_new = jnp.sum(q.astype(F32) * kn, axis=-1, keepdims=True)
        m_sc[...] = jnp.where(ok, s_new, NEG)
        l_sc[...] = jnp.where(ok, jnp.ones((n_rows, 1), F32), 0.0)
        vn = vnew_ref[0].astype(BF16).astype(F32)
        acc_sc[...] = jnp.where(ok, jnp.broadcast_to(vn, (n_rows, SEG)), 0.0)

    width = n_pages * page
    pos = g * width + lax.broadcasted_iota(I32, (1, width), 1)
    rel = (pos - past_len).astype(F32)
    s = jnp.concatenate([_dot(q, k_refs[i][0].astype(BF16)) for i in range(n_pages)], axis=-1)
    s = s + slope_ref[...] * rel + bias_ref[0]
    m_old = m_sc[...]
    m_new = jnp.maximum(m_old, jnp.max(s, axis=-1, keepdims=True))
    alpha = jnp.exp(m_old - m_new)
    p = jnp.exp(s - m_new)
    l_sc[...] = alpha * l_sc[...] + jnp.sum(p, axis=-1, keepdims=True)
    p = p.astype(BF16)
    pv = jnp.zeros((n_rows, SEG), F32)
    for i in range(n_pages):
        pi = p[:, i * page:(i + 1) * page]
        if v_transposed:
            pv = pv + _dot_nt(pi, v_refs[i][0].astype(BF16))
        else:
            n_h = v_refs[i].shape[1] // page
            pv = pv + jnp.concatenate(
                [_dot(pi, v_refs[i][0, pl.ds(h, page, stride=n_h), :].astype(BF16)) for h in range(n_h)], axis=-1)
    acc_sc[...] = alpha * acc_sc[...] + pv
    m_sc[...] = m_new

    @pl.when(g == pl.num_programs(1) - 1)
    def _():
        o_ref[0] = epilogue(acc_sc[...] / l_sc[...], *extra_refs)


def _diff_sample_epilogue(on, lam_ref, subg_ref, *, lam_init):
    row = lax.broadcasted_iota(I32, on.shape, 0)
    colh = lax.broadcasted_iota(I32, on.shape, 1) // (2 * D_HEAD)
    own = colh == row // 2
    o1 = jnp.sum(jnp.where(own & (row % 2 == 0), on, 0.0), axis=0, keepdims=True)
    o2 = jnp.sum(jnp.where(own & (row % 2 == 1), on, 0.0), axis=0, keepdims=True)
    lp = lam_ref[...]
    lam = (jnp.exp(jnp.sum(lp[0:1] * lp[1:2], axis=-1, keepdims=True))
           - jnp.exp(jnp.sum(lp[2:3] * lp[3:4], axis=-1, keepdims=True)) + lam_init)
    o = o1 - lam * o2
    w = 2 * D_HEAD
    segs = [_rms(o[:, h * w:(h + 1) * w]) * subg_ref[...] * (1.0 - lam_init) for h in range(H_DIFF)]
    return jnp.concatenate(segs, axis=-1)


def _dsa_sample_epilogue(on):
    row = lax.broadcasted_iota(I32, on.shape, 0)
    colh = lax.broadcasted_iota(I32, on.shape, 1) // D_HEAD
    return jnp.sum(jnp.where(colh == row, on, 0.0), axis=0, keepdims=True)


def _paged_attn(pt_flat, qmat, row_slope, k_new, v_new, newok, bias, k_pool, v_pool, v_transposed, extras,
                epilogue, n_pg, n_pages, page_base, name):
    DB, R, _ = qmat.shape
    page = k_pool.shape[2]
    assert page == LANES
    kern = functools.partial(_paged_attn_kernel, n_pages=n_pages, n_rows=R, past_len=n_pg * page,
                             v_transposed=v_transposed, epilogue=epilogue)

    def page_spec(i, pool):
        blk = (1,) + pool.shape[1:]
        zeros = (0,) * (pool.ndim - 1)
        return pl.BlockSpec(blk, lambda b, g, pt: (page_base + pt[b * n_pg + g * n_pages + i],) + zeros)

    per_b = lambda shp: pl.BlockSpec(shp, lambda b, g, pt: (b, 0, 0))
    grid_spec = pltpu.PrefetchScalarGridSpec(
        num_scalar_prefetch=1,
        grid=(DB, n_pg // n_pages),
        in_specs=[per_b((1, R, SEG)), pl.BlockSpec((R, 1), lambda b, g, pt: (0, 0)),
                  per_b((1, 1, SEG)), per_b((1, 1, SEG)), per_b((1, 1, 1)),
                  pl.BlockSpec((1, 1, n_pages * page), lambda b, g, pt: (b, 0, g))]
                 + [page_spec(i, k_pool) for i in range(n_pages)]
                 + [page_spec(i, v_pool) for i in range(n_pages)]
                 + [pl.BlockSpec(e.shape, lambda b, g, pt: (0, 0)) for e in extras],
        out_specs=per_b((1, 1, SEG)),
        scratch_shapes=[pltpu.VMEM((R, 1), F32), pltpu.VMEM((R, 1), F32), pltpu.VMEM((R, SEG), F32)],
    )
    return pl.pallas_call(
        kern,
        out_shape=jax.ShapeDtypeStruct((DB, 1, SEG), F32),
        grid_spec=grid_spec,
        compiler_params=_cparams(("parallel", "arbitrary")),
        name=name,
    )(pt_flat, qmat, row_slope, k_new, v_new, newok, bias,
      *([k_pool] * n_pages), *([v_pool] * n_pages), *extras)


def _block_rows(q, width):
    n = SEG // width
    own = (jnp.arange(SEG)[None, :] // width) == jnp.arange(n)[:, None]
    return jnp.where(own[None], q[:, None, :], jnp.zeros((), q.dtype))


def _outproj_kernel(od_ref, os_ref, x_ref, ga_ref, scf_ref, shf_ref, g_ref, wo_ref, wr_ref, br_ref,
                    x1_ref, h2_ref, e_ref, gate_ref, *, n_experts):
    o = _dot(od_ref[0], wo_ref[0:SEG, :]) + _dot(os_ref[0], wo_ref[SEG:2 * SEG, :])
    x1 = x_ref[0] + ga_ref[0] * o
    x1_ref[0] = x1
    h2 = _rms(x1) * g_ref[...] * (1.0 + scf_ref[0]) + shf_ref[0]
    h2_ref[0] = h2
    logits = _dot(h2.astype(BF16), wr_ref[...]) + br_ref[...]
    lane = lax.broadcasted_iota(I32, logits.shape, 1).astype(F32)
    lg = jnp.where(lane < n_experts, logits, -jnp.inf)
    vals, idxs = [], []
    for _ in range(TOP_K):
        m = jnp.max(lg, axis=-1, keepdims=True)
        idx = jnp.min(jnp.where(lg == m, lane, float(LANES)), axis=-1, keepdims=True)
        lg = jnp.where(lane == idx, -jnp.inf, lg)
        vals.append(m)
        idxs.append(idx)
    ex = [jnp.exp(v - vals[0]) for v in vals]
    tot = ex[0] + ex[1] + ex[2] + ex[3]
    e_ref[0] = jnp.concatenate(idxs, axis=-1).astype(I32)
    gate_ref[0] = jnp.concatenate([e / tot for e in ex], axis=-1)


def _outproj(od, osx, x, ga, scf, shf, g, wo, wr, br, n_experts, tm):
    B, S, D = x.shape
    mod_rows = ga.shape[1]
    mod_blk = (1, 1, D) if mod_rows == 1 else (1, tm, D)
    mod_map = (lambda b, i: (b, 0, 0)) if mod_rows == 1 else (lambda b, i: (b, i, 0))
    row = lambda w: pl.BlockSpec((1, tm, w), lambda b, i: (b, i, 0))
    const = lambda shp: pl.BlockSpec(shp, lambda b, i: (0,) * len(shp))
    kern = functools.partial(_outproj_kernel, n_experts=n_experts)
    return pl.pallas_call(
        kern,
        out_shape=[jax.ShapeDtypeStruct((B, S, D), F32), jax.ShapeDtypeStruct((B, S, D), F32),
                   jax.ShapeDtypeStruct((B, S, TOP_K), I32), jax.ShapeDtypeStruct((B, S, TOP_K), F32)],
        grid=(B, S // tm),
        in_specs=[row(SEG), row(SEG), row(D)] + [pl.BlockSpec(mod_blk, mod_map)] * 3
                 + [const((1, D)), const((2 * SEG, D)), const((D, LANES)), const((1, LANES))],
        out_specs=[row(D), row(D), row(TOP_K), row(TOP_K)],
        compiler_params=_cparams(("parallel", "parallel")),
        name="outproj_router",
    )(od, osx, x, ga, scf, shf, g, wo, wr, br)


def _rank_kernel(e_ref, rank_ref, cnt_ref, carry_sc, *, tr):
    @pl.when(pl.program_id(0) == 0)
    def _():
        carry_sc[...] = jnp.zeros(carry_sc.shape, F32)

    e = e_ref[...]
    lane = lax.broadcasted_iota(I32, (tr, LANES), 1)
    ohs = [lane == e[:, k:k + 1] for k in range(TOP_K)]
    oh = jnp.zeros((tr, LANES), F32)
    for m in ohs:
        oh = oh + jnp.where(m, 1.0, 0.0)
    r = lax.broadcasted_iota(I32, (tr, tr), 0)
    c = lax.broadcasted_iota(I32, (tr, tr), 1)
    lower = jnp.where(r > c, 1.0, 0.0).astype(BF16)
    rank_full = carry_sc[...] + _dot(lower, oh.astype(BF16))
    cols = [jnp.sum(jnp.where(m, rank_full, 0.0), axis=-1, keepdims=True) for m in ohs]
    rank_ref[...] = jnp.concatenate(cols, axis=-1).astype(I32)
    carry_sc[...] = carry_sc[...] + jnp.sum(oh, axis=0, keepdims=True)
    cnt_ref[...] = carry_sc[...]


def _ranks(e_idx, tr):
    T = e_idx.shape[0]
    kern = functools.partial(_rank_kernel, tr=tr)
    return pl.pallas_call(
        kern,
        out_shape=[jax.ShapeDtypeStruct((T, TOP_K), I32), jax.ShapeDtypeStruct((1, LANES), F32)],
        grid=(T // tr,),
        in_specs=[pl.BlockSpec((tr, TOP_K), lambda i: (i, 0))],
        out_specs=[pl.BlockSpec((tr, TOP_K), lambda i: (i, 0)), pl.BlockSpec((1, LANES), lambda i: (0, 0))],
        scratch_shapes=[pltpu.VMEM((1, LANES), F32)],
        compiler_params=_cparams(("arbitrary",)),
        name="moe_rank",
    )(e_idx)


def _row_copy(src_ref, src_row, dst_ref, dst_row, sem):
    return pltpu.make_async_copy(src_ref.at[pl.ds(src_row, 1)], dst_ref.at[pl.ds(dst_row, 1)], sem)


def _scatter_kernel(dest_ref, h_ref, xb_in_ref, xb_ref, sem, *, ts):
    del xb_in_ref

    def start(r, carry):
        for k in range(TOP_K):
            _row_copy(h_ref, r, xb_ref, dest_ref[r * TOP_K + k], sem).start(priority=k % 2)
        return carry

    def wait(r, carry):
        for k in range(TOP_K):
            _row_copy(h_ref, r, xb_ref, dest_ref[r * TOP_K + k], sem).wait()
        return carry

    lax.fori_loop(0, ts, start, 0)
    lax.fori_loop(0, ts, wait, 0)


def _scatter(dest_flat, h2, xb, ts):
    T, D = h2.shape
    kern = functools.partial(_scatter_kernel, ts=ts)
    return pl.pallas_call(
        kern,
        out_shape=jax.ShapeDtypeStruct(xb.shape, xb.dtype),
        grid=(T // ts,),
        in_specs=[pl.BlockSpec((ts * TOP_K,), lambda i: (i,), memory_space=pltpu.SMEM),
                  pl.BlockSpec((ts, D), lambda i: (i, 0)),
                  pl.BlockSpec(memory_space=pl.ANY)],
        out_specs=pl.BlockSpec(memory_space=pl.ANY),
        scratch_shapes=[pltpu.SemaphoreType.DMA(())],
        input_output_aliases={2: 0},
        compiler_params=pltpu.CompilerParams(dimension_semantics=("arbitrary",), vmem_limit_bytes=VMEM_LIMIT,
                                             has_side_effects=True),
        name="moe_scatter",
    )(dest_flat, h2, xb)


def _expert_kernel(blk_e_ref, n_used_ref, x_ref, wgu_ref, bgu_ref, wd_ref, bd_ref, y_ref, wgu_sc, wd_sc,
                   *, d_ff, chunk):
    j = pl.program_id(0)
    used = j < n_used_ref[0]
    new_expert = (j == 0) | (blk_e_ref[j] != blk_e_ref[jnp.maximum(j - 1, 0)])

    @pl.when(used & new_expert)
    def _():
        wgu_sc[...] = wgu_ref[0].astype(BF16)
        wd_sc[...] = wd_ref[0].astype(BF16)

    @pl.when(used)
    def _():
        x = x_ref[...].astype(BF16)
        acc = jnp.broadcast_to(bd_ref[0], y_ref.shape)
        for c in range(d_ff // chunk):
            lo = c * chunk
            g = _dot(x, wgu_sc[:, lo:lo + chunk]) + bgu_ref[0, :, lo:lo + chunk]
            u = _dot(x, wgu_sc[:, d_ff + lo:d_ff + lo + chunk]) + bgu_ref[0, :, d_ff + lo:d_ff + lo + chunk]
            g = jnp.minimum(g, SWIGLU_LIMIT)
            u = jnp.clip(u, -SWIGLU_LIMIT, SWIGLU_LIMIT)
            a = (u + 1.0) * (g * jax.nn.sigmoid(SWIGLU_ALPHA * g))
            acc = acc + _dot(a.astype(BF16), wd_sc[lo:lo + chunk, :])
        y_ref[...] = acc

    @pl.when(pl.program_id(0) >= n_used_ref[0])
    def _():
        y_ref[...] = jnp.zeros(y_ref.shape, F32)


def _experts(blk_e, n_used, xb, wgu, bgu, wd, bd, blk):
    n_rows, D = xb.shape
    E, _, two_ff = wgu.shape
    d_ff = two_ff // 2
    chunk = 512 if d_ff % 512 == 0 else d_ff
    kern = functools.partial(_expert_kernel, d_ff=d_ff, chunk=chunk)
    grid_spec = pltpu.PrefetchScalarGridSpec(
        num_scalar_prefetch=2,
        grid=(n_rows // blk,),
        in_specs=[pl.BlockSpec((blk, D), lambda j, be, nu: (j, 0)),
                  pl.BlockSpec((1, D, two_ff), lambda j, be, nu: (be[j], 0, 0)),
                  pl.BlockSpec((1, 1, two_ff), lambda j, be, nu: (be[j], 0, 0)),
                  pl.BlockSpec((1, d_ff, D), lambda j, be, nu: (be[j], 0, 0)),
                  pl.BlockSpec((1, 1, D), lambda j, be, nu: (be[j], 0, 0))],
        out_specs=pl.BlockSpec((blk, D), lambda j, be, nu: (j, 0)),
        scratch_shapes=[pltpu.VMEM((D, two_ff), BF16), pltpu.VMEM((d_ff, D), BF16)],
    )
    return pl.pallas_call(
        kern,
        out_shape=jax.ShapeDtypeStruct((n_rows, D), F32),
        grid_spec=grid_spec,
        compiler_params=_cparams(("arbitrary",)),
        name="moe_experts",
    )(blk_e, n_used, xb, wgu, bgu, wd, bd)


def _combine_kernel(dest_ref, yb_ref, gate_ref, x1_ref, gf_ref, o_ref, buf, sem, *, tc):
    def start(r, carry):
        for k in range(TOP_K):
            _row_copy(yb_ref, dest_ref[r * TOP_K + k], buf.at[k], r, sem).start(priority=k % 2)
        return carry

    def wait(r, carry):
        for k in range(TOP_K):
            _row_copy(yb_ref, dest_ref[r * TOP_K + k], buf.at[k], r, sem).wait()
        return carry

    lax.fori_loop(0, tc, start, 0)
    lax.fori_loop(0, tc, wait, 0)
    gate = gate_ref[0]
    y = gate[:, 0:1] * buf[0]
    for k in range(1, TOP_K):
        y = y + gate[:, k:k + 1] * buf[k]
    o_ref[0] = x1_ref[0] + gf_ref[0] * y


def _combine(dest_flat, yb, gate, x1, gf, tc):
    B, S, D = x1.shape
    nt = S // tc
    mod_rows = gf.shape[1]
    mod_blk = (1, 1, D) if mod_rows == 1 else (1, tc, D)
    mod_map = (lambda b, i: (b, 0, 0)) if mod_rows == 1 else (lambda b, i: (b, i, 0))
    kern = functools.partial(_combine_kernel, tc=tc)
    return pl.pallas_call(
        kern,
        out_shape=jax.ShapeDtypeStruct((B, S, D), F32),
        grid=(B, nt),
        in_specs=[pl.BlockSpec((tc * TOP_K,), lambda b, i: (b * nt + i,), memory_space=pltpu.SMEM),
                  pl.BlockSpec(memory_space=pl.ANY),
                  pl.BlockSpec((1, tc, TOP_K), lambda b, i: (b, i, 0)),
                  pl.BlockSpec((1, tc, D), lambda b, i: (b, i, 0)),
                  pl.BlockSpec(mod_blk, mod_map)],
        out_specs=pl.BlockSpec((1, tc, D), lambda b, i: (b, i, 0)),
        scratch_shapes=[pltpu.VMEM((TOP_K, tc, D), F32), pltpu.SemaphoreType.DMA(())],
        compiler_params=_cparams(("arbitrary", "arbitrary")),
        name="moe_combine",
    )(dest_flat, yb, gate, x1, gf)


def _moe(h2, e_idx, gate, x1, gf, wgu, bgu, wd, bd, blk, t_tile):
    B, S, D = x1.shape
    T = B * S
    E = wgu.shape[0]
    rank, counts = _ranks(e_idx.reshape(T, TOP_K), t_tile)
    counts = counts[0, :E].astype(I32)
    padded = (counts + blk - 1) // blk * blk
    pad_end = jnp.cumsum(padded)
    pad_start = pad_end - padded
    dest = (pad_start[e_idx.reshape(T, TOP_K)] + rank).reshape(T * TOP_K)
    n_blk = (T * TOP_K + E * (blk - 1) + blk - 1) // blk
    blk_start = jnp.arange(n_blk, dtype=I32) * blk
    blk_e = jnp.minimum(jnp.sum((pad_end[None, :] <= blk_start[:, None]).astype(I32), axis=1), E - 1)
    n_used = (pad_end[-1] // blk).astype(I32).reshape(1)
    xb = _scatter(dest, h2.reshape(T, D), jnp.zeros((n_blk * blk, D), F32), t_tile)
    yb = _experts(blk_e, n_used, xb, wgu, bgu, wd, bd, blk)
    return _combine(dest, yb, gate, x1, gf, t_tile)


def _alibi_slopes(n):
    return jnp.asarray([2.0 ** (-8.0 * (i + 1) / n) for i in range(n)], dtype=F32)


def _lambda_init(layer):
    return 0.8 - 0.6 * math.exp(-0.3 * layer)


def _tile(n, want):
    return want if n % want == 0 else n


def kernel(x_prompt, x_sample, cache_diff_k, cache_diff_v, cache_dsa_k, cache_dsa_v, cache_idx_k, page_table,
           c_prompt, c_sample, w_ada, b_ada, norm_attn_g, w_in, q_norm_diff_g, k_norm_diff_g, q_norm_dsa_g,
           k_norm_dsa_g, diff_lambda_qk, diff_subln_g, w_out, norm_ffn_g, w_router, b_router, w_gate_up,
           b_gate_up, w_down, b_down):
    depth = w_ada.shape[0]
    B, S, D = x_prompt.shape
    DB, DQ, _ = x_sample.shape
    assert DQ == 1, "sample path implements single-token decode"
    n_pool, page = cache_diff_k.shape[1:3]
    n_pg = page_table.shape[1]
    E = w_router.shape[-1]
    d_ff = w_down.shape[2]
    slopes_diff = _alibi_slopes(H_DIFF)
    slopes_dsa = _alibi_slopes(H_DSA)
    pt_flat = page_table.reshape(-1).astype(I32)
    pages_per_step = next(n for n in (16, 8, 4, 2, 1) if n_pg % n == 0)
    idx_pages_per_step = next(n for n in (64, 32, 16, 8, 4, 2, 1) if n_pg % n == 0)

    pool_dk = jnp.transpose(cache_diff_k, (0, 1, 3, 4, 5, 2)).reshape(depth * n_pool, SEG, page)
    pool_dv = cache_diff_v.reshape(depth * n_pool, page * H_DIFF, 2 * D_HEAD)
    pool_sk = jnp.transpose(cache_dsa_k, (0, 1, 3, 4, 2)).reshape(depth * n_pool, SEG, page)
    pool_sv = jnp.transpose(cache_dsa_v, (0, 1, 3, 4, 2)).reshape(depth * n_pool, SEG, page)
    pool_ik = jnp.transpose(cache_idx_k, (0, 1, 3, 2)).reshape(depth * n_pool, D_IDX, page)

    head_block = (jnp.arange(SEG)[:, None] // D_HEAD) == (jnp.arange(SEG)[None, :] // D_HEAD)
    bd = jnp.where(head_block, 1.0 / D_HEAD, 0.0).astype(BF16)

    xp = x_prompt
    xs = x_sample.reshape(1, DB, D)
    new_p = [[] for _ in range(5)]
    new_s = [[] for _ in range(5)]
    tm = _tile(S, 512)
    for l in range(depth):
        lam_init = _lambda_init(l)
        mod = _ada(jnp.concatenate([c_prompt, c_sample], axis=0), w_ada[l], b_ada[l])
        mod_p = [m.reshape(B, 1, D) for m in jnp.split(mod[:B], 6, axis=-1)]
        mod_s = [m.reshape(1, DB, D) for m in jnp.split(mod[B:], 6, axis=-1)]

        w_pad = jnp.concatenate(
            [w_in[l][:, :OFF_IK + D_IDX], jnp.zeros((D, LANES - D_IDX), F32),
             w_in[l][:, OFF_IK + D_IDX:], jnp.zeros((D, LANES - H_IDX), F32)], axis=1).astype(BF16)
        tile8 = lambda g: jnp.tile(g, SEG // D_HEAD)
        gains = jnp.stack([tile8(q_norm_diff_g[l]), tile8(k_norm_diff_g[l]),
                           tile8(q_norm_dsa_g[l]), tile8(k_norm_dsa_g[l])])
        g_attn = norm_attn_g[l].reshape(1, D)
        lam_p = diff_lambda_qk[l]
        subg = diff_subln_g[l].reshape(1, 2 * D_HEAD)
        wo = w_out[l].astype(BF16)
        wr = jnp.zeros((D, LANES), F32).at[:, :E].set(w_router[l]).astype(BF16)
        br =jnp.zeros((1, LANES), F32).at[0, :E].set(b_router[l])
        g_ffn = norm_ffn_g[l].reshape(1, D)
        wgu = w_gate_up[l]
        bgu = b_gate_up[l].reshape(E, 1, 2 * d_ff)
        wd = w_down[l]
        bdn = b_down[l].reshape(E, 1, D)

        (dq, dk, dkb, dv, dvb, sq, sk, skb, sv, svb, iq, ik, ikb, iw) = _inproj(
            xp, mod_p[1], mod_p[0], g_attn, w_pad, bd, gains, tm)
        o_diff = _diff_prompt(slopes_diff, dq, dkb, dvb, lam_p, subg, lam_init, _tile(S, 512), _tile(S, 512))
        o_dsa = _dsa_prompt(slopes_dsa, sq, iq, iw, skb, svb, ikb, _tile(S, 256), _tile(S, 1024))
        x1, h2, e_idx, gate = _outproj(o_diff, o_dsa, xp, mod_p[2], mod_p[4], mod_p[3], g_ffn,
                                       wo, wr, br, E, tm)
        xp = _moe(h2, e_idx, gate, x1, mod_p[5], wgu, bgu, wd, bdn, 512, _tile(S, 256))
        for j, a in enumerate((dk.reshape(B, S, H_DIFF, 2, D_HEAD), dv.reshape(B, S, H_DIFF, 2 * D_HEAD),
                               sk.reshape(B, S, H_DSA, D_HEAD), sv.reshape(B, S, H_DSA, D_HEAD), ik)):
            new_p[j].append(a)

        (dq, dk, dkb, dv, dvb, sq, sk, skb, sv, svb, iq, ik, ikb, iw) = _inproj(
            xs, mod_s[1], mod_s[0], g_attn, w_pad, bd, gains, DB)
        base = l * n_pool
        scores, score_new = _sample_index(
            pt_flat, iq.reshape(DB, H_IDX, D_IDX), iw.reshape(DB, H_IDX, 1), ik.reshape(DB, 1, D_IDX),
            pool_ik, n_pg, idx_pages_per_step, base)
        bias, newok = _sample_select(scores.reshape(DB, n_pg * page), score_new.reshape(DB, 1),
                                     _tile(n_pg * page, 512))
        row_slope_diff = jnp.repeat(slopes_diff, 2).reshape(2 * H_DIFF, 1)
        o_diff = _paged_attn(
            pt_flat, _block_rows(dq[0], D_HEAD), row_slope_diff, dk.reshape(DB, 1, SEG), dv.reshape(DB, 1, SEG),
            jnp.ones((DB, 1, 1), F32), jnp.zeros((DB, 1, n_pg * page), F32), pool_dk, pool_dv, False,
            (lam_p, subg), functools.partial(_diff_sample_epilogue, lam_init=lam_init),
            n_pg, pages_per_step, base, "diff_sample")
        o_dsa = _paged_attn(
            pt_flat, _block_rows(sq[0], D_HEAD), slopes_dsa.reshape(H_DSA, 1), sk.reshape(DB, 1, SEG),
            sv.reshape(DB, 1, SEG), newok.reshape(DB, 1, 1), bias.reshape(DB, 1, n_pg * page), pool_sk, pool_sv, True,
            (), _dsa_sample_epilogue, n_pg, pages_per_step, base, "dsa_sample")
        x1, h2, e_idx, gate = _outproj(o_diff.reshape(1, DB, SEG).astype(BF16), o_dsa.reshape(1, DB, SEG).astype(BF16),
                                       xs, mod_s[2], mod_s[4], mod_s[3], g_ffn, wo, wr, br, E, DB)
        xs = _moe(h2, e_idx, gate, x1, mod_s[5], wgu, bgu, wd, bdn, 128, DB)
        for j, a in enumerate((dk.reshape(DB, 1, H_DIFF, 2, D_HEAD), dv.reshape(DB, 1, H_DIFF, 2 * D_HEAD),
                               sk.reshape(DB, 1, H_DSA, D_HEAD), sv.reshape(DB, 1, H_DSA, D_HEAD),
                               ik.reshape(DB, 1, D_IDX))):
            new_s[j].append(a)

    return (xp, xs.reshape(DB, 1, D),
            jnp.stack(new_p[0]), jnp.stack(new_p[1]), jnp.stack(new_p[2]), jnp.stack(new_p[3]), jnp.stack(new_p[4]),
            jnp.stack(new_s[0]), jnp.stack(new_s[1]), jnp.stack(new_s[2]), jnp.stack(new_s[3]), jnp.stack(new_s[4]))
```
